```python
import math
import jax, jax.numpy as jnp
from jax import lax
import numpy as np

D_MODEL = 1024
BATCH = 8
SEQ = 2048
DEPTH = 4

N_MIXERS = 3
N_MLA = (DEPTH + 2) // 3
N_MOBA = (DEPTH + 1) // 3
N_SSD = DEPTH // 3
NORM_EPS = 1e-6
NEG_INF = -1e30
MAX_POS_OFFSET = 4096

MLA_HEADS = 16
MLA_Q_RANK = 256
MLA_KV_RANK = 256
MLA_NOPE = 128
MLA_ROPE = 64
MLA_V = 128
ROPE_BASE = 10000.0
ATTN_QBLOCK = 128

MOBA_HEADS = 8
MOBA_HEAD_DIM = D_MODEL // MOBA_HEADS
MOBA_BLOCK = 256
MOBA_TOPK = 3
MOBA_QCHUNK = 16

REL_BUCKETS = 32
REL_MAX_DIST = 128

SSD_INNER = 2 * D_MODEL
SSD_HEAD_DIM = 64
SSD_HEADS = SSD_INNER // SSD_HEAD_DIM
SSD_GROUPS = 2
SSD_HEADS_PER_GROUP = SSD_HEADS // SSD_GROUPS
SSD_STATE = 128
SSD_CONV = 4
SSD_CONV_DIM = SSD_INNER + 2 * SSD_GROUPS * SSD_STATE
SSD_IN_DIM = SSD_INNER + SSD_CONV_DIM + SSD_HEADS
SSD_CHUNK = 128

MLP_HIDDEN = 4 * D_MODEL

kernel_name = 'hybrid_mla_moba_ssd_trunk'


def rmsnorm(x, g):
    xf = x.astype(jnp.float32)
    y = xf * lax.rsqrt(jnp.mean(xf * xf, axis=-1, keepdims=True) + NORM_EPS)
    return (y * g.astype(jnp.float32)).astype(x.dtype)


def rope_angles(positions):
    inv_freq = ROPE_BASE ** (-(jnp.arange(0, MLA_ROPE, 2, dtype=jnp.float32) / MLA_ROPE))
    ang = positions.astype(jnp.float32)[..., None] * inv_freq
    return jnp.cos(ang), jnp.sin(ang)


def apply_rope(x, cos, sin):
    x1, x2 = jnp.split(x.astype(jnp.float32), 2, axis=-1)
    return jnp.concatenate([x1 * cos - x2 * sin, x2 * cos + x1 * sin], axis=-1).astype(x.dtype)


def t5_bucket(dist):
    n = jnp.maximum(dist, 0)
    max_exact = REL_BUCKETS // 2
    nf = jnp.maximum(n, max_exact).astype(jnp.float32)
    large = max_exact + (jnp.log(nf / max_exact) / math.log(REL_MAX_DIST / max_exact)
                         * (REL_BUCKETS - max_exact)).astype(jnp.int32)
    large = jnp.minimum(large, REL_BUCKETS - 1)
    return jnp.where(n < max_exact, n, large)


def mla_mixer(h, positions, w_in, q_norm, w_uq, kv_norm, w_ukv, w_o):
    B, S, _ = h.shape
    proj = h @ w_in
    c_q, c_kv, k_pe = jnp.split(proj, [MLA_Q_RANK, MLA_Q_RANK + MLA_KV_RANK], axis=-1)
    q = (rmsnorm(c_q, q_norm) @ w_uq).reshape(B, S, MLA_HEADS, MLA_NOPE + MLA_ROPE)
    q_nope, q_pe = jnp.split(q, [MLA_NOPE], axis=-1)
    kv = (rmsnorm(c_kv, kv_norm) @ w_ukv).reshape(B, S, MLA_HEADS, MLA_NOPE + MLA_V)
    k_nope, v = jnp.split(kv, [MLA_NOPE], axis=-1)
    cos, sin = rope_angles(positions)
    q_pe = apply_rope(q_pe, cos[:, :, None], sin[:, :, None])
    k_pe = apply_rope(k_pe, cos, sin)
    scale = (MLA_NOPE + MLA_ROPE) ** -0.5
    nqb = S // ATTN_QBLOCK
    key_idx = jnp.arange(S)

    def block(args):
        qn, qp, start = args
        logits = (jnp.einsum('bqhd,bkhd->bhqk', qn, k_nope)
                  + jnp.einsum('bqhr,bkr->bhqk', qp, k_pe)).astype(jnp.float32) * scale
        q_idx = start + jnp.arange(ATTN_QBLOCK)
        causal = key_idx[None, :] <= q_idx[:, None]
        p = jax.nn.softmax(jnp.where(causal, logits, NEG_INF), axis=-1)
        return jnp.einsum('bhqk,bkhd->bqhd', p.astype(v.dtype), v)

    def to_blocks(t):
        return jnp.moveaxis(t.reshape(B, nqb, ATTN_QBLOCK, *t.shape[2:]), 1, 0)

    out = lax.map(block, (to_blocks(q_nope), to_blocks(q_pe), jnp.arange(nqb) * ATTN_QBLOCK))
    out = jnp.moveaxis(out, 0, 1).reshape(B, S, MLA_HEADS * MLA_V)
    return out @ w_o


def moba_mixer(h, w_qkv, w_o, rel_bias):
    B, S, _ = h.shape
    H, Dh, BLK = MOBA_HEADS, MOBA_HEAD_DIM, MOBA_BLOCK
    s_pad = -(-S // BLK) * BLK
    nb = s_pad // BLK
    k_sel = min(MOBA_TOPK, nb)
    qkv = (h @ w_qkv).reshape(B, S, 3, H, Dh)
    qkv = jnp.pad(qkv, ((0, 0), (0, s_pad - S), (0, 0), (0, 0), (0, 0)))
    q = jnp.moveaxis(qkv[:, :, 0], 1, 2)
    k = jnp.moveaxis(qkv[:, :, 1], 1, 2)
    v = jnp.moveaxis(qkv[:, :, 2], 1, 2)
    kb = k.reshape(B, H, nb, BLK, Dh)
    vb = v.reshape(B, H, nb, BLK, Dh)
    k_mean = jnp.mean(kb.astype(jnp.float32), axis=3)
    gate = jnp.einsum('bhsd,bhnd->bhsn', q.astype(jnp.float32), k_mean)
    q_blk = jnp.arange(s_pad) // BLK
    past = jnp.arange(nb)[None, :] < q_blk[:, None]
    _, sel = lax.top_k(jnp.where(past, gate, NEG_INF), k_sel)
    sel_valid = sel < q_blk[:, None]
    bias_tab = rel_bias.T
    scale = Dh ** -0.5
    nc = s_pad // MOBA_QCHUNK
    gather_blocks = jax.vmap(jax.vmap(lambda blocks, idx: blocks[idx]))
    head_idx = jnp.arange(H)[None, :, None, None, None]

    def chunk(args):
        qc, sc, vc, c = args
        start = c * MOBA_QCHUNK
        q_idx = start + jnp.arange(MOBA_QCHUNK)
        k_g = gather_blocks(kb, sc)
        v_g = gather_blocks(vb, sc)
        key_idx = sc[..., None] * BLK + jnp.arange(BLK)
        bias_g = bias_tab[head_idx, t5_bucket(q_idx[:, None, None] - key_idx)]
        logit_g = jnp.einsum('bhqd,bhqjkd->bhqjk', qc, k_g).astype(jnp.float32) * scale + bias_g
        logit_g = jnp.where(vc[..., None], logit_g, NEG_INF).reshape(B, H, MOBA_QCHUNK, k_sel * BLK)
        own = (start // BLK) * BLK
        k_o = lax.dynamic_slice_in_dim(k, own, BLK, axis=2)
        v_o = lax.dynamic_slice_in_dim(v, own, BLK, axis=2)
        dist_o = q_idx[:, None] - (own + jnp.arange(BLK))[None, :]
        logit_o = (jnp.einsum('bhqd,bhkd->bhqk', qc, k_o).astype(jnp.float32) * scale
                   + bias_tab[:, t5_bucket(dist_o)])
        logit_o = jnp.where(dist_o >= 0, logit_o, NEG_INF)
        p = jax.nn.softmax(jnp.concatenate([logit_g, logit_o], axis=-1), axis=-1).astype(v.dtype)
        p_g = p[..., :k_sel * BLK].reshape(B, H, MOBA_QCHUNK, k_sel, BLK)
        p_o = p[..., k_sel * BLK:]
        return (jnp.einsum('bhqjk,bhqjkd->bhqd', p_g, v_g)
                + jnp.einsum('bhqk,bhkd->bhqd', p_o, v_o))

    def to_chunks(t):
        return jnp.moveaxis(t.reshape(B, H, nc, MOBA_QCHUNK, *t.shape[3:]), 2, 0)

    out = lax.map(chunk, (to_chunks(q), to_chunks(sel), to_chunks(sel_valid), jnp.arange(nc)))
    out = out.transpose(1, 0, 3, 2, 4).reshape(B, s_pad, H * Dh)[:, :S]
    return out @ w_o


def causal_depthwise_conv(x, w, b):
    C = x.shape[-1]
    y = lax.conv_general_dilated(x, w[:, None, :], window_strides=(1,),
                                 padding=[(SSD_CONV - 1, 0)],
                                 dimension_numbers=('NWC', 'WIO', 'NWC'),
                                 feature_group_count=C)
    return y + b


def ssd_scan(xh, dt, a, bm, cm):
    B, S = xh.shape[:2]
    L = SSD_CHUNK
    nc = S // L

    def chunks(t):
        return t.reshape(B, nc, L, *t.shape[2:])

    x_dt = chunks(xh * dt[..., None])
    a_cs = jnp.cumsum(chunks(dt * a), axis=2)
    bc = chunks(bm)
    cc = chunks(cm)
    tril = jnp.arange(L)[:, None] >= jnp.arange(L)[None, :]
    seg = a_cs[:, :, :, None] - a_cs[:, :, None, :]
    decay_in = jnp.exp(jnp.where(tril[None, None, :, :, None, None], seg, NEG_INF))
    cb = jnp.einsum('bclgn,bcsgn->bclsg', cc, bc)
    y_diag = jnp.einsum('bclsg,bclsgr,bcsgrp->bclgrp', cb, decay_in, x_dt)
    decay_out = jnp.exp(a_cs[:, :, -1:] - a_cs)
    states = jnp.einsum('bclgn,bclgr,bclgrp->bcgrpn', bc, decay_out, x_dt)
    chunk_decay = jnp.exp(a_cs[:, :, -1])

    def step(state, inp):
        st, dec = inp
        return dec[..., None, None] * state + st, state

    init = jnp.zeros_like(states[:, 0])
    _, prev = lax.scan(step, init, (jnp.moveaxis(states, 1, 0), jnp.moveaxis(chunk_decay, 1, 0)))
    prev = jnp.moveaxis(prev, 0, 1)
    y_off = jnp.einsum('bclgn,bcgrpn,bclgr->bclgrp', cc, prev, jnp.exp(a_cs))
    return (y_diag + y_off).reshape(B, S, *xh.shape[2:])


def ssd_mixer(h, w_in, conv_w, conv_b, dt_bias, a_log, d_skip, norm_w, w_out):
    B, S, _ = h.shape
    proj = h @ w_in
    z, xbc, dt = jnp.split(proj, [SSD_INNER, SSD_INNER + SSD_CONV_DIM], axis=-1)
    xbc = jax.nn.silu(causal_depthwise_conv(xbc, conv_w, conv_b))
    xs, bm, cm = jnp.split(xbc, [SSD_INNER, SSD_INNER + SSD_GROUPS * SSD_STATE], axis=-1)
    dt = jax.nn.softplus(dt.astype(jnp.float32) + dt_bias.astype(jnp.float32))
    a = -jnp.exp(a_log.astype(jnp.float32))
    grp = (SSD_GROUPS, SSD_HEADS_PER_GROUP)
    xh = xs.reshape(B, S, *grp, SSD_HEAD_DIM)
    y = ssd_scan(xh, dt.reshape(B, S, *grp), a.reshape(grp),
                 bm.reshape(B, S, SSD_GROUPS, SSD_STATE), cm.reshape(B, S, SSD_GROUPS, SSD_STATE))
    y = y + d_skip.astype(jnp.float32).reshape(*grp, 1) * xh
    g = y * jax.nn.silu(z.astype(jnp.float32)).reshape(B, S, *grp, SSD_HEAD_DIM)
    g = g.reshape(B, S, SSD_GROUPS, SSD_INNER // SSD_GROUPS)
    g = g * lax.rsqrt(jnp.mean(g * g, axis=-1, keepdims=True) + NORM_EPS)
    g = (g.reshape(B, S, SSD_INNER) * norm_w.astype(jnp.float32)).astype(h.dtype)
    return g @ w_out


def sq_relu_mlp(h, w1, w2):
    a = jax.nn.relu(h @ w1)
    return (a * a) @ w2


def setup_inputs(seed: int = 0) -> dict:
    key = jax.random.key(seed)
    ks = iter(jax.random.split(key, 40))
    f32 = jnp.float32
    out_scale = (2 * DEPTH) ** -0.5

    def dense(shape, fan_in, scale=1.0):
        return jax.random.normal(next(ks), shape, f32) * (scale * fan_in ** -0.5)

    def gain(shape):
        return 1.0 + 0.05 * jax.random.normal(next(ks), shape, f32)

    x = jax.random.normal(next(ks), (BATCH, SEQ, D_MODEL), f32)
    offsets = jax.random.randint(next(ks), (BATCH, 1), 0, MAX_POS_OFFSET, dtype=jnp.int32)
    positions = offsets + jnp.arange(SEQ, dtype=jnp.int32)[None, :]
    u = jax.random.uniform(next(ks), (N_SSD, SSD_HEADS), f32)
    dt0 = jnp.exp(u * (math.log(0.1) - math.log(0.001)) + math.log(0.001))
    return {
        'x': x,
        'positions': positions,
        'norm_mix': gain((DEPTH, D_MODEL)),
        'norm_mlp': gain((DEPTH, D_MODEL)),
        'norm_final': gain((D_MODEL,)),
        'rel_bias': 0.2 * jax.random.normal(next(ks), (REL_BUCKETS, MOBA_HEADS), f32),
        'mla_w_in': dense((N_MLA, D_MODEL, MLA_Q_RANK + MLA_KV_RANK + MLA_ROPE), D_MODEL),
        'mla_q_norm': gain((N_MLA, MLA_Q_RANK)),
        'mla_w_uq': dense((N_MLA, MLA_Q_RANK, MLA_HEADS * (MLA_NOPE + MLA_ROPE)), MLA_Q_RANK),
        'mla_kv_norm': gain((N_MLA, MLA_KV_RANK)),
        'mla_w_ukv': dense((N_MLA, MLA_KV_RANK, MLA_HEADS * (MLA_NOPE + MLA_V)), MLA_KV_RANK),
        'mla_w_o': dense((N_MLA, MLA_HEADS * MLA_V, D_MODEL), MLA_HEADS * MLA_V, out_scale),
        'moba_w_qkv': dense((N_MOBA, D_MODEL, 3 * D_MODEL), D_MODEL),
        'moba_w_o': dense((N_MOBA, D_MODEL, D_MODEL), D_MODEL, out_scale),
        'ssd_w_in': dense((N_SSD, D_MODEL, SSD_IN_DIM), D_MODEL),
        'ssd_conv_w': dense((N_SSD, SSD_CONV, SSD_CONV_DIM), SSD_CONV),
        'ssd_conv_b': 0.02 * jax.random.normal(next(ks), (N_SSD, SSD_CONV_DIM), f32),
        'ssd_dt_bias': dt0 + jnp.log(-jnp.expm1(-dt0)),
        'ssd_a_log': jnp.log(jax.random.uniform(next(ks), (N_SSD, SSD_HEADS), f32, 1.0, 16.0)),
        'ssd_d': gain((N_SSD, SSD_HEADS)),
        'ssd_norm': gain((N_SSD, SSD_INNER)),
        'ssd_w_out': dense((N_SSD, SSD_INNER, D_MODEL), SSD_INNER, out_scale),
        'mlp_w1': dense((DEPTH, D_MODEL, MLP_HIDDEN), D_MODEL),
        'mlp_w2': dense((DEPTH, MLP_HIDDEN, D_MODEL), MLP_HIDDEN, out_scale),
    }


def reference(x, positions, norm_mix, norm_mlp, norm_final, rel_bias,
              mla_w_in, mla_q_norm, mla_w_uq, mla_kv_norm, mla_w_ukv, mla_w_o,
              moba_w_qkv, moba_w_o,
              ssd_w_in, ssd_conv_w, ssd_conv_b, ssd_dt_bias, ssd_a_log, ssd_d, ssd_norm, ssd_w_out,
              mlp_w1, mlp_w2):
    for i in range(DEPTH):
        kind, j = i % N_MIXERS, i // N_MIXERS
        h = rmsnorm(x, norm_mix[i])
        if kind == 0:
            m = mla_mixer(h, positions, mla_w_in[j], mla_q_norm[j], mla_w_uq[j],
                          mla_kv_norm[j], mla_w_ukv[j], mla_w_o[j])
        elif kind == 1:
            m = moba_mixer(h, moba_w_qkv[j], moba_w_o[j], rel_bias)
        else:
            m = ssd_mixer(h, ssd_w_in[j], ssd_conv_w[j], ssd_conv_b[j], ssd_dt_bias[j],
                          ssd_a_log[j], ssd_d[j], ssd_norm[j], ssd_w_out[j])
        x = x + m.astype(x.dtype)
        x = x + sq_relu_mlp(rmsnorm(x, norm_mlp[i]), mlp_w1[i], mlp_w2[i]).astype(x.dtype)
    return rmsnorm(x, norm_final)
```

```python
import functools
import math

import numpy as np
import jax
import jax.numpy as jnp
from jax import lax
from jax.experimental import pallas as pl
from jax.experimental.pallas import tpu as pltpu

F32 = jnp.float32
BF16 = jnp.bfloat16
HIGHEST = lax.Precision.HIGHEST

D_MODEL = 1024
DEPTH = 4
N_MIXERS = 3
NORM_EPS = 1e-6
NEG_INF = -1e30

MLA_HEADS = 16
MLA_Q_RANK = 256
MLA_KV_RANK = 256
MLA_NOPE = 128
MLA_ROPE = 64
MLA_V = 128
ROPE_BASE = 10000.0

MOBA_HEADS = 8
MOBA_HEAD_DIM = D_MODEL // MOBA_HEADS
MOBA_BLOCK = 256
MOBA_TOPK = 3
REL_BUCKETS = 32
REL_MAX_DIST = 128

SSD_INNER = 2 * D_MODEL
SSD_HEAD_DIM = 64
SSD_HEADS = SSD_INNER // SSD_HEAD_DIM
SSD_GROUPS = 2
SSD_HEADS_PER_GROUP = SSD_HEADS // SSD_GROUPS
SSD_STATE = 128
SSD_CONV = 4
SSD_CONV_DIM = SSD_INNER + 2 * SSD_GROUPS * SSD_STATE
SSD_CHUNK = 128

MLP_HIDDEN = 4 * D_MODEL

LANES = 128
SUBLANES = 8
MIB = 1024 * 1024

NT_DIMS = (((1,), (1,)), ((), ()))
TN_DIMS = (((0,), (0,)), ((), ()))


def _params(semantics, vmem_mib=48):
    return pltpu.CompilerParams(dimension_semantics=semantics,
                                vmem_limit_bytes=vmem_mib * MIB)


def _rms(x, g):
    ms = jnp.mean(x * x, axis=-1, keepdims=True)
    return x * lax.rsqrt(ms + NORM_EPS) * g


def _norm_proj_kernel(x_ref, g_ref, w_ref, *o_refs, splits, col_chunk):
    h = _rms(x_ref[...], g_ref[...]).astype(BF16)
    off = 0
    for o_ref, n in zip(o_refs, splits):
        for c0 in range(0, n, col_chunk):
            c1 = min(c0 + col_chunk, n)
            o_ref[:, c0:c1] = jnp.dot(
                h, w_ref[:, off + c0:off + c1],
                preferred_element_type=F32).astype(o_ref.dtype)
        off += n


def _norm_proj(x, g, w, splits, dtypes, tm):
    t, d = x.shape
    n = w.shape[1]
    assert sum(splits) == n and t % tm == 0
    return pl.pallas_call(
        functools.partial(_norm_proj_kernel, splits=tuple(splits), col_chunk=1024),
        grid=(t // tm,),
        in_specs=[pl.BlockSpec((tm, d), lambda i: (i, 0)),
                  pl.BlockSpec((1, d), lambda i: (0, 0)),
                  pl.BlockSpec((d, n), lambda i: (0, 0))],
        out_specs=[pl.BlockSpec((tm, s), lambda i: (i, 0)) for s in splits],
        out_shape=[jax.ShapeDtypeStruct((t, s), dt) for s, dt in zip(splits, dtypes)],
        compiler_params=_params(("parallel",), 56),
        name="norm_proj",
    )(x, g.reshape(1, d), w)


def _rope_table_kernel(pos_ref, freq_ref, cos_ref, sin_ref):
    ang = pos_ref[...].astype(F32) * freq_ref[...]
    cos_ref[...] = jnp.cos(ang)
    sin_ref[...] = jnp.sin(ang)


def _rope_tables(positions):
    b, s = positions.shape
    t = b * s
    half = MLA_ROPE // 2
    per_row = LANES // half
    rows = t // per_row
    inv_freq = ROPE_BASE ** (-(jnp.arange(0, MLA_ROPE, 2, dtype=F32) / MLA_ROPE))
    pos_rep = jnp.repeat(positions.reshape(rows, per_row), half, axis=1)
    freq = jnp.tile(inv_freq, per_row).reshape(1, LANES)
    tr = 512
    cos, sin = pl.pallas_call(
        _rope_table_kernel,
        grid=(rows // tr,),
        in_specs=[pl.BlockSpec((tr, LANES), lambda i: (i, 0)),
                  pl.BlockSpec((1, LANES), lambda i: (0, 0))],
        out_specs=[pl.BlockSpec((tr, LANES), lambda i: (i, 0))] * 2,
        out_shape=[jax.ShapeDtypeStruct((rows, LANES), F32)] * 2,
        compiler_params=_params(("parallel",)),
        name="rope_tables",
    )(pos_rep, freq)
    cos = cos.reshape(t, half)
    sin = sin.reshape(t, half)
    zeros = lambda n: jnp.zeros((t, n), F32)
    cos_t = jnp.concatenate([cos, cos, zeros(LANES - 2 * half)], axis=1)
    sin_a = jnp.concatenate([-sin, zeros(LANES - half)], axis=1)
    sin_b = jnp.concatenate([zeros(half), sin, zeros(LANES - 2 * half)], axis=1)
    return cos_t, sin_a, sin_b


def _rope128(v, cos_t, sin_a, sin_b):
    half = MLA_ROPE // 2
    return (v * cos_t + pltpu.roll(v, LANES - half, 1) * sin_a
            + pltpu.roll(v, half, 1) * sin_b)


MLA_QW = 2 * LANES
MLA_KVW = MLA_NOPE + MLA_V
MLA_IN_PAD = MLA_Q_RANK + MLA_KV_RANK + LANES


def _mla_proj_kernel(x_ref, g_ref, win_ref, qn_ref, kvn_ref, wuq_ref, wukv_ref,
                     cos_ref, sa_ref, sb_ref, q_ref, kv_ref, kpe_ref):
    h = _rms(x_ref[...], g_ref[...]).astype(BF16)
    proj = jnp.dot(h, win_ref[...], preferred_element_type=F32)
    cq = _rms(proj[:, :MLA_Q_RANK], qn_ref[...]).astype(BF16)
    ckv = _rms(proj[:, MLA_Q_RANK:MLA_Q_RANK + MLA_KV_RANK], kvn_ref[...]).astype(BF16)
    cos_t, sin_a, sin_b = cos_ref[...], sa_ref[...], sb_ref[...]
    kpe_ref[...] = _rope128(proj[:, MLA_Q_RANK + MLA_KV_RANK:], cos_t, sin_a, sin_b).astype(BF16)
    for hh in range(MLA_HEADS):
        c0 = hh * MLA_QW
        qh = jnp.dot(cq, wuq_ref[:, c0:c0 + MLA_QW], preferred_element_type=F32)
        q_ref[:, c0:c0 + MLA_NOPE] = qh[:, :MLA_NOPE].astype(BF16)
        q_ref[:, c0 + MLA_NOPE:c0 + MLA_QW] = _rope128(
            qh[:, MLA_NOPE:], cos_t, sin_a, sin_b).astype(BF16)
    n_kv = MLA_HEADS * MLA_KVW
    for c0 in range(0, n_kv, 1024):
        kv_ref[:, c0:c0 + 1024] = jnp.dot(
            ckv, wukv_ref[:, c0:c0 + 1024], preferred_element_type=F32).astype(BF16)


def _mla_proj(x, g, w_in, q_norm, w_uq, kv_norm, w_ukv, rope, tm=256):
    t, d = x.shape
    nq = MLA_HEADS * MLA_QW
    nkv = MLA_HEADS * MLA_KVW
    w_in_p = jnp.pad(w_in, ((0, 0), (0, MLA_IN_PAD - w_in.shape[1]))).astype(BF16)
    w_uq_p = jnp.pad(w_uq.reshape(MLA_Q_RANK, MLA_HEADS, MLA_NOPE + MLA_ROPE),
                     ((0, 0), (0, 0), (0, MLA_QW - MLA_NOPE - MLA_ROPE))
                     ).reshape(MLA_Q_RANK, nq).astype(BF16)
    row = lambda i: (i, 0)
    fixed = lambda i: (0, 0)
    return pl.pallas_call(
        _mla_proj_kernel,
        grid=(t // tm,),
        in_specs=[pl.BlockSpec((tm, d), row),
                  pl.BlockSpec((1, d), fixed),
                  pl.BlockSpec((d, MLA_IN_PAD), fixed),
                  pl.BlockSpec((1, MLA_Q_RANK), fixed),
                  pl.BlockSpec((1, MLA_KV_RANK), fixed),
                  pl.BlockSpec((MLA_Q_RANK, nq), fixed),
                  pl.BlockSpec((MLA_KV_RANK, nkv), fixed),
                  pl.BlockSpec((tm, LANES), row),
                  pl.BlockSpec((tm, LANES), row),
                  pl.BlockSpec((tm, LANES), row)],
        out_specs=[pl.BlockSpec((tm, nq), row),
                   pl.BlockSpec((tm, nkv), row),
                   pl.BlockSpec((tm, LANES), row)],
        out_shape=[jax.ShapeDtypeStruct((t, nq), BF16),
                   jax.ShapeDtypeStruct((t, nkv), BF16),
                   jax.ShapeDtypeStruct((t, LANES), BF16)],
        compiler_params=_params(("parallel",)),
        name="mla_proj",
    )(x, g.reshape(1, d), w_in_p, q_norm.reshape(1, -1), kv_norm.reshape(1, -1),
      w_uq_p, w_ukv.astype(BF16), *rope)


def _softmax_step(s, m, l, acc, v):
    m_new = jnp.maximum(m, jnp.max(s, axis=0, keepdims=True))
    alpha = jnp.exp(m - m_new)
    p = jnp.exp(s - m_new)
    l_new = alpha * l + jnp.sum(p, axis=0, keepdims=True)
    pv = lax.dot_general(v, p.astype(BF16), TN_DIMS, preferred_element_type=F32)
    return m_new, l_new, alpha * acc + pv


def _mla_attn_kernel(q_ref, kv_ref, kpe_ref, o_ref, *, hp, tq, scale):
    i = pl.program_id(2)
    q0 = pl.multiple_of(i * tq, tq)

    def chunk(carry, k0, masked):
        kpe = kpe_ref[pl.ds(k0, tq), :]
        out = []
        for hh in range(hp):
            m, l, acc = carry[hh]
            c0 = hh * MLA_KVW
            kc = jnp.concatenate([kv_ref[pl.ds(k0, tq), c0:c0 + MLA_NOPE], kpe], axis=1)
            v = kv_ref[pl.ds(k0, tq), c0 + MLA_NOPE:c0 + MLA_KVW]
            q = q_ref[:, hh * MLA_QW:(hh + 1) * MLA_QW]
            s = lax.dot_general(kc, q, NT_DIMS, preferred_element_type=F32) * scale
            if masked:
                kidx = lax.broadcasted_iota(jnp.int32, (tq, tq), 0)
                qidx = lax.broadcasted_iota(jnp.int32, (tq, tq), 1)
                s = jnp.where(kidx <= qidx, s, NEG_INF)
            out.append(_softmax_step(s, m, l, acc, v))
        return tuple(out)

    init = tuple((jnp.full((1, tq), NEG_INF, F32), jnp.zeros((1, tq), F32),
                  jnp.zeros((MLA_V, tq), F32)) for _ in range(hp))
    carry = lax.fori_loop(
        0, i, lambda j, c: chunk(c, pl.multiple_of(j * tq, tq), False), init)
    carry = chunk(carry, q0, True)
    for hh in range(hp):
        _, l, acc = carry[hh]
        o_ref[:, hh * MLA_V:(hh + 1) * MLA_V] = (acc / l).T.astype(BF16)


def _mla_attn(q, kv, kpe, b, s, hp=2, tq=512):
    t = b * s
    nq = s // tq
    groups = MLA_HEADS // hp
    scale = (MLA_NOPE + MLA_ROPE) ** -0.5
    return pl.pallas_call(
        functools.partial(_mla_attn_kernel, hp=hp, tq=tq, scale=scale),
        grid=(b, groups, nq),
        in_specs=[pl.BlockSpec((tq, hp * MLA_QW), lambda bb, g, i: (bb * nq + i, g)),
                  pl.BlockSpec((s, hp * MLA_KVW), lambda bb, g, i: (bb, g)),
                  pl.BlockSpec((s, LANES), lambda bb, g, i: (bb, 0))],
        out_specs=pl.BlockSpec((tq, hp * MLA_V), lambda bb, g, i: (bb * nq + i, g)),
        out_shape=jax.ShapeDtypeStruct((t, MLA_HEADS * MLA_V), BF16),
        compiler_params=_params(("parallel", "parallel", "arbitrary")),
        name="mla_attn",
    )(q, kv, kpe)


def _t5_bucket_np(n):
    max_exact = REL_BUCKETS // 2
    nf = np.maximum(n, max_exact).astype(np.float32)
    large = max_exact + (np.log(nf / np.float32(max_exact))
                         / np.float32(math.log(REL_MAX_DIST / max_exact))
                         * np.float32(REL_BUCKETS - max_exact)).astype(np.int32)
    large = np.minimum(large, REL_BUCKETS - 1)
    return np.where(n < max_exact, n, large).astype(np.int32)


def _moba_bias_kernel(tab_ref, bucket_ref, o_ref):
    h = pl.program_id(0)
    bucket = bucket_ref[...]
    acc = jnp.zeros(bucket.shape, F32)
    for k in range(REL_BUCKETS):
        acc = jnp.where(bucket == k, tab_ref[k, h], acc)
    o_ref[0] = acc


def _moba_bias(rel_bias):
    blk = MOBA_BLOCK
    key = np.arange(blk)[:, None]
    qry = np.arange(blk)[None, :]
    dist = np.stack([np.maximum(qry - key, 0), blk + qry - key])
    bucket = jnp.asarray(_t5_bucket_np(dist))
    return pl.pallas_call(
        _moba_bias_kernel,
        grid=(MOBA_HEADS,),
        in_specs=[pl.BlockSpec(memory_space=pltpu.SMEM),
                  pl.BlockSpec((2, blk, blk), lambda h: (0, 0, 0))],
        out_specs=pl.BlockSpec((1, 2, blk, blk), lambda h: (h, 0, 0, 0)),
        out_shape=jax.ShapeDtypeStruct((MOBA_HEADS, 2, blk, blk), F32),
        compiler_params=_params(("arbitrary",)),
        name="moba_bias",
    )(rel_bias, bucket)


def _moba_attn_kernel(tab_ref, q_ref, k_ref, v_ref, bias_ref, o_ref, sel_ref, *, hp, nb, scale):
    g = pl.program_id(1)
    i = pl.program_id(2)
    blk = MOBA_BLOCK
    dh = MOBA_HEAD_DIM
    s_len = nb * blk
    q0 = pl.multiple_of(i * blk, blk)

    @pl.when(i == 0)
    def _():
        nidx = lax.broadcasted_iota(jnp.int32, (nb, s_len), 0)
        qblk = lax.broadcasted_iota(jnp.int32, (nb, s_len), 1) // blk
        past = nidx < qblk
        for hh in range(hp):
            cs = slice(hh * dh, (hh + 1) * dh)
            k_mean = jnp.concatenate(
                [jnp.sum(k_ref[n * blk:(n + 1) * blk, cs].astype(F32), axis=0, keepdims=True)
                 for n in range(nb)], axis=0) / blk
            gate = lax.dot_general(k_mean, q_ref[:, cs].astype(F32), NT_DIMS,
                                   precision=HIGHEST, preferred_element_type=F32)
            gm = jnp.where(past, gate, NEG_INF)
            rank = jnp.zeros((nb, s_len), F32)
            for m in range(nb):
                gm_m = gm[m:m + 1, :]
                tie = jnp.where(gm_m == gm, jnp.where(nidx > m, 1.0, 0.0), 0.0)
                rank = rank + jnp.where(gm_m > gm, 1.0, tie)
            sel = jnp.where(rank < min(MOBA_TOPK, nb), jnp.where(past, 1.0, 0.0), 0.0)
            for qb in range(nb):
                sel_ref[hh, qb] = sel[:, qb * blk:(qb + 1) * blk]

    kidx = lax.broadcasted_iota(jnp.int32, (blk, blk), 0)
    qidx = lax.broadcasted_iota(jnp.int32, (blk, blk), 1)
    carry = []
    for hh in range(hp):
        cs = slice(hh * dh, (hh + 1) * dh)
        q = q_ref[pl.ds(q0, blk), cs]
        s = lax.dot_general(k_ref[pl.ds(q0, blk), cs], q, NT_DIMS,
                            preferred_element_type=F32) * scale + bias_ref[hh, 0]
        s = jnp.where(kidx <= qidx, s, NEG_INF)
        carry.append(_softmax_step(
            s, jnp.full((1, blk), NEG_INF, F32), jnp.zeros((1, blk), F32),
            jnp.zeros((dh, blk), F32), v_ref[pl.ds(q0, blk), cs]))

    def past_block(j, carry):
        k0 = pl.multiple_of(j * blk, blk)
        adjacent = (i - j) == 1
        out = []
        for hh in range(hp):
            m, l, acc = carry[hh]
            cs = slice(hh * dh, (hh + 1) * dh)
            q = q_ref[pl.ds(q0, blk), cs]
            far_bias = tab_ref[REL_BUCKETS - 1, g * hp + hh]
            bias = jnp.where(adjacent, bias_ref[hh, 1], far_bias)
            s = lax.dot_general(k_ref[pl.ds(k0, blk), cs], q, NT_DIMS,
                                preferred_element_type=F32) * scale + bias
            sel = sel_ref[hh, i, pl.ds(j, 1), :]
            s = jnp.where(sel > 0.5, s, NEG_INF)
            out.append(_softmax_step(s, m, l, acc, v_ref[pl.ds(k0, blk), cs]))
        return tuple(out)

    carry = lax.fori_loop(0, i, past_block, tuple(carry))
    for hh in range(hp):
        _, l, acc = carry[hh]
        o_ref[:, hh * dh:(hh + 1) * dh] = (acc / l).T.astype(BF16)


def _moba_attn(qkv, bias, rel_bias, b, s, hp=4):
    assert MOBA_BLOCK + 1 >= REL_MAX_DIST and s % MOBA_BLOCK == 0
    t = b * s
    blk = MOBA_BLOCK
    nb = s // blk
    dh = MOBA_HEAD_DIM
    groups = MOBA_HEADS // hp
    return pl.pallas_call(
        functools.partial(_moba_attn_kernel, hp=hp, nb=nb, scale=dh ** -0.5),
        grid=(b, groups, nb),
        in_specs=[pl.BlockSpec(memory_space=pltpu.SMEM),
                  pl.BlockSpec((s, hp * dh), lambda bb, g, i: (bb, g)),
                  pl.BlockSpec((s, hp * dh), lambda bb, g, i: (bb, groups + g)),
                  pl.BlockSpec((s, hp * dh), lambda bb, g, i: (bb, 2 * groups + g)),
                  pl.BlockSpec((hp, 2, blk, blk), lambda bb, g, i: (g, 0, 0, 0))],
        out_specs=pl.BlockSpec((blk, hp * dh), lambda bb, g, i: (bb * nb + i, g)),
        out_shape=jax.ShapeDtypeStruct((t, MOBA_HEADS * dh), BF16),
        scratch_shapes=[pltpu.VMEM((hp, nb, nb, blk), F32)],
        compiler_params=_params(("parallel", "parallel", "arbitrary")),
        name="moba_attn",
    )(rel_bias, qkv, qkv, qkv, bias)


SSD_GN = SSD_GROUPS * SSD_STATE
SSD_GW = SSD_INNER // SSD_GROUPS


def _softplus(x):
    return jnp.maximum(x, 0.0) + jnp.log1p(jnp.exp(-jnp.abs(x)))


def _silu(x):
    return x * jax.nn.sigmoid(x)


def _expand_heads(v, e3):
    h1 = v.astype(BF16)
    r1 = v - h1.astype(F32)
    h2 = r1.astype(BF16)
    h3 = (r1 - h2.astype(F32)).astype(BF16)
    return jnp.dot(jnp.concatenate([h1, h2, h3], axis=1), e3, preferred_element_type=F32)


def _ssd_kernel(z_ref, xbc_ref, dt_ref, cw_ref, cb_ref, dtb_ref, alog_ref, dexp_ref,
                nw_ref, e3_ref, o_ref, xpad_ref, st_ref):
    c = pl.program_id(1)
    L = SSD_CHUNK
    P = SSD_HEAD_DIM
    N = SSD_STATE
    halo = SUBLANES

    @pl.when(c == 0)
    def _():
        xpad_ref[0:halo, :] = jnp.zeros((halo, SSD_CONV_DIM), F32)
        st_ref[...] = jnp.zeros_like(st_ref)

    xpad_ref[halo:halo + L, :] = xbc_ref[...]
    conv = cb_ref[...] + cw_ref[SSD_CONV - 1:SSD_CONV, :] * xbc_ref[...]
    for k in range(SSD_CONV - 1):
        r0 = halo - (SSD_CONV - 1) + k
        conv = conv + cw_ref[k:k + 1, :] * xpad_ref[r0:r0 + L, :]
    xpad_ref[0:halo, :] = xpad_ref[L:L + halo, :]
    xc = _silu(conv)
    xs = xc[:, :SSD_INNER]
    bm = xc[:, SSD_INNER:SSD_INNER + SSD_GN].astype(BF16)
    cm = xc[:, SSD_INNER + SSD_GN:].astype(BF16)
    xs_b = xs.astype(BF16)

    dt = _softplus(dt_ref[...] + dtb_ref[...])
    a = -jnp.exp(alog_ref[...])
    row = lax.broadcasted_iota(jnp.int32, (L, L), 0)
    col = lax.broadcasted_iota(jnp.int32, (L, L), 1)
    tril = row >= col
    a_cs = jnp.dot(jnp.where(tril, 1.0, 0.0).astype(F32), dt * a,
                   precision=HIGHEST, preferred_element_type=F32)
    a_cs_t = a_cs.T
    dt_t = dt.T
    e3 = e3_ref[...]
    decay_exp = _expand_heads(jnp.exp(a_cs), e3)
    w_exp = _expand_heads(jnp.exp(a_cs[L - 1:L, :] - a_cs) * dt, e3)
    xw = (xs * w_exp).astype(BF16)
    lane = lax.broadcasted_iota(jnp.int32, (L, LANES), 1)

    ys = []
    for g in range(SSD_GROUPS):
        gs = slice(g * SSD_GW, (g + 1) * SSD_GW)
        b_g = bm[:, g * N:(g + 1) * N]
        c_g = cm[:, g * N:(g + 1) * N]
        cb = lax.dot_general(c_g, b_g, NT_DIMS, preferred_element_type=F32)
        state = st_ref[:, gs]
        y_off = jnp.dot(c_g, state.astype(BF16), preferred_element_type=F32) * decay_exp[:, gs]
        new_state = lax.dot_general(b_g, xw[:, gs], TN_DIMS, preferred_element_type=F32)
        st_ref[:, gs] = state * decay_exp[L - 1:L, gs] + new_state
        y_diag = []
        for pr in range(SSD_HEADS_PER_GROUP // 2):
            h0 = g * SSD_HEADS_PER_GROUP + 2 * pr
            mats = []
            for hr in (h0, h0 + 1):
                seg = a_cs[:, hr:hr + 1] - a_cs_t[hr:hr + 1, :]
                decay = jnp.exp(jnp.where(tril, seg, NEG_INF))
                mats.append((cb * decay * dt_t[hr:hr + 1, :]).astype(BF16))
            xp = xs_b[:, h0 * P:(h0 + 2) * P]
            zero = jnp.zeros_like(xp)
            x2 = jnp.concatenate([jnp.where(lane < P, xp, zero),
                                  jnp.where(lane >= P, xp, zero)], axis=0)
            y_diag.append(jnp.dot(jnp.concatenate(mats, axis=1), x2,
                                  preferred_element_type=F32))
        ys.append(jnp.concatenate(y_diag, axis=1) + y_off)
    y = jnp.concatenate(ys, axis=1) + dexp_ref[...] * xs
    gated = y * _silu(z_ref[...])
    outs = []
    for g in range(SSD_GROUPS):
        gg = gated[:, g * SSD_GW:(g + 1) * SSD_GW]
        ms = jnp.mean(gg * gg, axis=-1, keepdims=True)
        outs.append(gg * lax.rsqrt(ms + NORM_EPS))
    o_ref[...] = (jnp.concatenate(outs, axis=1) * nw_ref[...]).astype(BF16)


def _ssd_scan(z, xbc, dt, conv_w, conv_b, dt_bias, a_log, d_skip, norm_w, b, s):
    t = b * s
    L = SSD_CHUNK
    nc = s // L
    pad_h = lambda v: jnp.pad(v, (0, LANES - SSD_HEADS)).reshape(1, LANES)
    e = np.zeros((LANES, SSD_INNER), np.float32)
    for r in range(SSD_HEADS):
        e[r, r * SSD_HEAD_DIM:(r + 1) * SSD_HEAD_DIM] = 1.0
    e3 = jnp.asarray(np.concatenate([e, e, e], axis=0), dtype=BF16)
    d_exp = jnp.repeat(d_skip, SSD_HEAD_DIM).reshape(1, SSD_INNER)
    row = lambda bb, c: (bb * nc + c, 0)
    fixed = lambda bb, c: (0, 0)
    return pl.pallas_call(
        _ssd_kernel,
        grid=(b, nc),
        in_specs=[pl.BlockSpec((L, SSD_INNER), row),
                  pl.BlockSpec((L, SSD_CONV_DIM), row),
                  pl.BlockSpec((L, LANES), row),
                  pl.BlockSpec((SSD_CONV, SSD_CONV_DIM), fixed),
                  pl.BlockSpec((1, SSD_CONV_DIM), fixed),
                  pl.BlockSpec((1, LANES), fixed),
                  pl.BlockSpec((1, LANES), fixed),
                  pl.BlockSpec((1, SSD_INNER), fixed),
                  pl.BlockSpec((1, SSD_INNER), fixed),
                  pl.BlockSpec((3 * LANES, SSD_INNER), fixed)],
        out_specs=pl.BlockSpec((L, SSD_INNER), row),
        out_shape=jax.ShapeDtypeStruct((t, SSD_INNER), BF16),
        scratch_shapes=[pltpu.VMEM((L + SUBLANES, SSD_CONV_DIM), F32),
                        pltpu.VMEM((SSD_STATE, SSD_INNER), F32)],
        compiler_params=_params(("parallel", "arbitrary")),
        name="ssd_scan",
    )(z, xbc, dt, conv_w, conv_b.reshape(1, -1), pad_h(dt_bias), pad_h(a_log), d_exp,
      norm_w.reshape(1, -1), e3)


def _post_kernel(x_ref, a_ref, wo_ref, g_ref, w1_ref, w2_ref, gf_ref, o_ref,
                 x1_ref, h_ref, acc_ref, *, final_norm):
    j = pl.program_id(1)

    @pl.when(j == 0)
    def _():
        x1 = x_ref[...] + jnp.dot(a_ref[...], wo_ref[...], preferred_element_type=F32)
        x1_ref[...] = x1
        h_ref[...] = _rms(x1, g_ref[...]).astype(BF16)
        acc_ref[...] = jnp.zeros_like(acc_ref)

    u = jnp.maximum(jnp.dot(h_ref[...], w1_ref[...], preferred_element_type=F32), 0.0)
    acc_ref[...] += jnp.dot((u * u).astype(BF16), w2_ref[...], preferred_element_type=F32)

    @pl.when(j == pl.num_programs(1) - 1)
    def _():
        y = x1_ref[...] + acc_ref[...]
        if final_norm:
            y = _rms(y, gf_ref[...])
        o_ref[...] = y


def _post(x, a, w_o, g, w1, w2, g_final, final_norm, tm=512, th=1024):
    t, d = x.shape
    ka = a.shape[1]
    hid = w1.shape[1]
    return pl.pallas_call(
        functools.partial(_post_kernel, final_norm=final_norm),
        grid=(t // tm, hid // th),
        in_specs=[pl.BlockSpec((tm, d), lambda i, j: (i, 0)),
                  pl.BlockSpec((tm, ka), lambda i, j: (i, 0)),
                  pl.BlockSpec((ka, d), lambda i, j: (0, 0)),
                  pl.BlockSpec((1, d), lambda i, j: (0, 0)),
                  pl.BlockSpec((d, th), lambda i, j: (0, j)),
                  pl.BlockSpec((th, d), lambda i, j: (j, 0)),
                  pl.BlockSpec((1, d), lambda i, j: (0, 0))],
        out_specs=pl.BlockSpec((tm, d), lambda i, j: (i, 0)),
        out_shape=jax.ShapeDtypeStruct((t, d), F32),
        scratch_shapes=[pltpu.VMEM((tm, d), F32), pltpu.VMEM((tm, d), BF16),
                        pltpu.VMEM((tm, d), F32)],
        compiler_params=_params(("parallel", "arbitrary"), 56),
        name="post_mlp",
    )(x, a, w_o.astype(BF16), g.reshape(1, d), w1.astype(BF16), w2.astype(BF16),
      g_final.reshape(1, d))


def _mla_mixer(x, g, rope, w_in, q_norm, w_uq, kv_norm, w_ukv, b, s):
    q, kv, kpe = _mla_proj(x, g, w_in, q_norm, w_uq, kv_norm, w_ukv, rope)
    return _mla_attn(q, kv, kpe, b, s)


def _moba_mixer(x, g, w_qkv, bias, rel_bias, b, s):
    n = w_qkv.shape[1]
    (qkv,) = _norm_proj(x, g, w_qkv.astype(BF16), (n,), (BF16,), tm=512)
    return _moba_attn(qkv, bias, rel_bias, b, s)


def _ssd_mixer(x, g, w_in, conv_w, conv_b, dt_bias, a_log, d_skip, norm_w, b, s):
    w_pad = jnp.pad(w_in, ((0, 0), (0, LANES - SSD_HEADS))).astype(BF16)
    z, xbc, dt = _norm_proj(x, g, w_pad, (SSD_INNER, SSD_CONV_DIM, LANES),
                            (F32, F32, F32), tm=256)
    return _ssd_scan(z, xbc, dt, conv_w, conv_b, dt_bias, a_log, d_skip, norm_w, b, s)


def kernel(x, positions, norm_mix, norm_mlp, norm_final, rel_bias, mla_w_in, mla_q_norm, mla_w_uq, mla_kv_norm, mla_w_ukv, mla_w_o, moba_w_qkv, moba_w_o, ssd_w_in, ssd_conv_w, ssd_conv_b, ssd_dt_bias, ssd_a_log, ssd_d, ssd_norm, ssd_w_out, mlp_w1, mlp_w2):
    b, s, d = x.shape
    xf = x.reshape(b * s, d)
    rope = None
    bias = None
    for i in range(DEPTH):
        kind, j = i % N_MIXERS, i // N_MIXERS
        if kind == 0:
            if rope is None:
                rope = _rope_tables(positions)
            a = _mla_mixer(xf, norm_mix[i], rope, mla_w_in[j], mla_q_norm[j], mla_w_uq[j],
                           mla_kv_norm[j], mla_w_ukv[j], b, s)
            w_o = mla_w_o[j]
        elif kind == 1:
            if bias is None:
                bias = _moba_bias(rel_bias)
            a = _moba_mixer(xf, norm_mix[i], moba_w_qkv[j], bias, rel_bias, b, s)
            w_o = moba_w_o[j]
        else:
            a = _ssd_mixer(xf, norm_mix[i], ssd_w_in[j], ssd_conv_w[j], ssd_conv_b[j],
                           ssd_dt_bias[j], ssd_a_log[j], ssd_d[j], ssd_norm[j], b, s)
            w_o = ssd_w_out[j]
        xf = _post(xf, a, w_o, norm_mlp[i], mlp_w1[i], mlp_w2[i], norm_final,
                   final_norm=(i == DEPTH - 1))
    return xf.reshape(b, s, d)
```

```python
import functools
import math

import numpy as np
import jax
import jax.numpy as jnp
from jax import lax
from jax.experimental import pallas as pl
from jax.experimental.pallas import tpu as pltpu

F32 = jnp.float32
BF16 = jnp.bfloat16
HIGHEST = lax.Precision.HIGHEST

D_MODEL = 1024
DEPTH = 4
N_MIXERS = 3
NORM_EPS = 1e-6
NEG_INF = -1e30

MLA_HEADS = 16
MLA_Q_RANK = 256
MLA_KV_RANK = 256
MLA_NOPE = 128
MLA_ROPE = 64
MLA_V = 128
ROPE_BASE = 10000.0

MOBA_HEADS = 8
MOBA_HEAD_DIM = D_MODEL // MOBA_HEADS
MOBA_BLOCK = 256
MOBA_TOPK = 3
REL_BUCKETS = 32
REL_MAX_DIST = 128

SSD_INNER = 2 * D_MODEL
SSD_HEAD_DIM = 64
SSD_HEADS = SSD_INNER // SSD_HEAD_DIM
SSD_GROUPS = 2
SSD_HEADS_PER_GROUP = SSD_HEADS // SSD_GROUPS
SSD_STATE = 128
SSD_CONV = 4
SSD_CONV_DIM = SSD_INNER + 2 * SSD_GROUPS * SSD_STATE
SSD_CHUNK = 128

MLP_HIDDEN = 4 * D_MODEL

LANES = 128
SUBLANES = 8
MIB = 1024 * 1024

NT_DIMS = (((1,), (1,)), ((), ()))
TN_DIMS = (((0,), (0,)), ((), ()))


def _params(semantics, vmem_mib=48):
    return pltpu.CompilerParams(dimension_semantics=semantics,
                                vmem_limit_bytes=vmem_mib * MIB)


def _rms(x, g):
    ms = jnp.mean(x * x, axis=-1, keepdims=True)
    return x * lax.rsqrt(ms + NORM_EPS) * g


def _norm_proj_kernel(x_ref, g_ref, w_ref, *o_refs, splits, col_chunk):
    h = _rms(x_ref[...], g_ref[...]).astype(BF16)
    off = 0
    for o_ref, n in zip(o_refs, splits):
        for c0 in range(0, n, col_chunk):
            c1 = min(c0 + col_chunk, n)
            o_ref[:, c0:c1] = jnp.dot(
                h, w_ref[:, off + c0:off + c1],
                preferred_element_type=F32).astype(o_ref.dtype)
        off += n


def _norm_proj(x, g, w, splits, dtypes, tm):
    t, d = x.shape
    n = w.shape[1]
    assert sum(splits) == n and t % tm == 0
    return pl.pallas_call(
        functools.partial(_norm_proj_kernel, splits=tuple(splits), col_chunk=1024),
        grid=(t // tm,),
        in_specs=[pl.BlockSpec((tm, d), lambda i: (i, 0)),
                  pl.BlockSpec((1, d), lambda i: (0, 0)),
                  pl.BlockSpec((d, n), lambda i: (0, 0))],
        out_specs=[pl.BlockSpec((tm, s), lambda i: (i, 0)) for s in splits],
        out_shape=[jax.ShapeDtypeStruct((t, s), dt) for s, dt in zip(splits, dtypes)],
        compiler_params=_params(("parallel",), 56),
        name="norm_proj",
    )(x, g.reshape(1, d), w)


def _rope_table_kernel(pos_ref, freq_ref, cos_ref, sin_ref):
    ang = pos_ref[...].astype(F32) * freq_ref[...]
    cos_ref[...] = jnp.cos(ang)
    sin_ref[...] = jnp.sin(ang)


def _rope_tables(positions):
    b, s = positions.shape
    t = b * s
    half = MLA_ROPE // 2
    per_row = LANES // half
    rows = t // per_row
    inv_freq = ROPE_BASE ** (-(jnp.arange(0, MLA_ROPE, 2, dtype=F32) / MLA_ROPE))
    pos_rep = jnp.repeat(positions.reshape(rows, per_row), half, axis=1)
    freq = jnp.tile(inv_freq, per_row).reshape(1, LANES)
    tr = 512
    cos, sin = pl.pallas_call(
        _rope_table_kernel,
        grid=(rows // tr,),
        in_specs=[pl.BlockSpec((tr, LANES), lambda i: (i, 0)),
                  pl.BlockSpec((1, LANES), lambda i: (0, 0))],
        out_specs=[pl.BlockSpec((tr, LANES), lambda i: (i, 0))] * 2,
        out_shape=[jax.ShapeDtypeStruct((rows, LANES), F32)] * 2,
        compiler_params=_params(("parallel",)),
        name="rope_tables",
    )(pos_rep, freq)
    cos = cos.reshape(t, half)
    sin = sin.reshape(t, half)
    zeros = lambda n: jnp.zeros((t, n), F32)
    cos_t = jnp.concatenate([cos, cos, zeros(LANES - 2 * half)], axis=1)
    sin_a = jnp.concatenate([-sin, zeros(LANES - half)], axis=1)
    sin_b = jnp.concatenate([zeros(half), sin, zeros(LANES - 2 * half)], axis=1)
    return cos_t, sin_a, sin_b


def _rope128(v, cos_t, sin_a, sin_b):
    half = MLA_ROPE // 2
    return (v * cos_t + pltpu.roll(v, LANES - half, 1) * sin_a
            + pltpu.roll(v, half, 1) * sin_b)


MLA_QW = 2 * LANES
MLA_KVW = MLA_NOPE + MLA_V
MLA_IN_PAD = MLA_Q_RANK + MLA_KV_RANK + LANES


def _mla_proj_kernel(x_ref, g_ref, win_ref, qn_ref, kvn_ref, wuq_ref, wukv_ref,
                     cos_ref, sa_ref, sb_ref, q_ref, kv_ref, kpe_ref):
    h = _rms(x_ref[...], g_ref[...]).astype(BF16)
    proj = jnp.dot(h, win_ref[...], preferred_element_type=F32)
    cq = _rms(proj[:, :MLA_Q_RANK], qn_ref[...]).astype(BF16)
    ckv = _rms(proj[:, MLA_Q_RANK:MLA_Q_RANK + MLA_KV_RANK], kvn_ref[...]).astype(BF16)
    cos_t, sin_a, sin_b = cos_ref[...], sa_ref[...], sb_ref[...]
    kpe_ref[...] = _rope128(proj[:, MLA_Q_RANK + MLA_KV_RANK:], cos_t, sin_a, sin_b).astype(BF16)
    for hh in range(MLA_HEADS):
        c0 = hh * MLA_QW
        qh = jnp.dot(cq, wuq_ref[:, c0:c0 + MLA_QW], preferred_element_type=F32)
        q_ref[:, c0:c0 + MLA_NOPE] = qh[:, :MLA_NOPE].astype(BF16)
        q_ref[:, c0 + MLA_NOPE:c0 + MLA_QW] = _rope128(
            qh[:, MLA_NOPE:], cos_t, sin_a, sin_b).astype(BF16)
    n_kv = MLA_HEADS * MLA_KVW
    for c0 in range(0, n_kv, 1024):
        kv_ref[:, c0:c0 + 1024] = jnp.dot(
            ckv, wukv_ref[:, c0:c0 + 1024], preferred_element_type=F32).astype(BF16)


def _mla_proj(x, g, w_in, q_norm, w_uq, kv_norm, w_ukv, rope, tm=256):
    t, d = x.shape
    nq = MLA_HEADS * MLA_QW
    nkv = MLA_HEADS * MLA_KVW
    w_in_p = jnp.pad(w_in, ((0, 0), (0, MLA_IN_PAD - w_in.shape[1]))).astype(BF16)
    w_uq_p = jnp.pad(w_uq.reshape(MLA_Q_RANK, MLA_HEADS, MLA_NOPE + MLA_ROPE),
                     ((0, 0), (0, 0), (0, MLA_QW - MLA_NOPE - MLA_ROPE))
                     ).reshape(MLA_Q_RANK, nq).astype(BF16)
    w_ukv_p = jnp.swapaxes(w_ukv.reshape(MLA_KV_RANK, MLA_HEADS, 2, MLA_NOPE), 1, 2
                           ).reshape(MLA_KV_RANK, nkv).astype(BF16)
    row = lambda i: (i, 0)
    fixed = lambda i: (0, 0)
    return pl.pallas_call(
        _mla_proj_kernel,
        grid=(t // tm,),
        in_specs=[pl.BlockSpec((tm, d), row),
                  pl.BlockSpec((1, d), fixed),
                  pl.BlockSpec((d, MLA_IN_PAD), fixed),
                  pl.BlockSpec((1, MLA_Q_RANK), fixed),
                  pl.BlockSpec((1, MLA_KV_RANK), fixed),
                  pl.BlockSpec((MLA_Q_RANK, nq), fixed),
                  pl.BlockSpec((MLA_KV_RANK, nkv), fixed),
                  pl.BlockSpec((tm, LANES), row),
                  pl.BlockSpec((tm, LANES), row),
                  pl.BlockSpec((tm, LANES), row)],
        out_specs=[pl.BlockSpec((tm, nq), row),
                   pl.BlockSpec((tm, nkv), row),
                   pl.BlockSpec((tm, LANES), row)],
        out_shape=[jax.ShapeDtypeStruct((t, nq), BF16),
                   jax.ShapeDtypeStruct((t, nkv), BF16),
                   jax.ShapeDtypeStruct((t, LANES), BF16)],
        compiler_params=_params(("parallel",)),
        name="mla_proj",
    )(x, g.reshape(1, d), w_in_p, q_norm.reshape(1, -1), kv_norm.reshape(1, -1),
      w_uq_p, w_ukv_p, *rope)


LOG2E = math.log2(math.e)


def _softmax_step(s, m, l, acc, v, scale=1.0):
    c = scale * LOG2E
    m_new = jnp.maximum(m, jnp.max(s, axis=0, keepdims=True))
    alpha = jnp.exp2((m - m_new) * c)
    p = jnp.exp2((s - m_new) * c)
    l_new = alpha * l + jnp.sum(p, axis=0, keepdims=True)
    pv = lax.dot_general(v, p.astype(BF16), TN_DIMS, preferred_element_type=F32)
    return m_new, l_new, alpha * acc + pv


def _mla_attn_kernel(q_ref, k_ref, kpe_ref, v_ref, o_ref, s_ref, m_ref, p_ref,
                     *, hp, tq, nq, scale):
    n = pl.program_id(0)
    c = scale * LOG2E
    kidx = lax.broadcasted_iota(jnp.int32, (tq, tq), 0)
    qidx = lax.broadcasted_iota(jnp.int32, (tq, tq), 1)
    causal = kidx <= qidx

    @pl.when(n == 0)
    def _():
        s_ref[...] = jnp.zeros_like(s_ref)
        m_ref[...] = jnp.zeros_like(m_ref)

    def scores(qi, slot):
        past = qi * tq
        kpe = kpe_ref[0:past + tq, :]
        for hh in range(hp):
            kc = jnp.concatenate(
                [k_ref[0:past + tq, hh * MLA_NOPE:(hh + 1) * MLA_NOPE], kpe], axis=1)
            q = q_ref[:, hh * MLA_QW:(hh + 1) * MLA_QW]
            s_diag = jnp.where(
                causal, lax.dot_general(kc[past:], q, NT_DIMS, preferred_element_type=F32),
                NEG_INF)
            s_ref[slot, hh, past:past + tq, :] = s_diag
            m = jnp.max(s_diag, axis=0, keepdims=True)
            if past:
                s_past = lax.dot_general(kc[:past], q, NT_DIMS, preferred_element_type=F32)
                s_ref[slot, hh, 0:past, :] = s_past
                m = jnp.maximum(m, jnp.max(s_past, axis=0, keepdims=True))
            m_ref[slot, hh] = m

    def attend(qi, slot):
        kv_len = (qi + 1) * tq
        for hh in range(hp):
            p = jnp.exp2((s_ref[slot, hh, 0:kv_len, :] - m_ref[slot, hh]) * c)
            l = jnp.sum(p, axis=0, keepdims=True)
            p_ref[hh, 0:kv_len, :] = p.astype(BF16)
            acc = lax.dot_general(v_ref[0:kv_len, hh * MLA_V:(hh + 1) * MLA_V],
                                  p_ref[hh, 0:kv_len, :], TN_DIMS, preferred_element_type=F32)
            o_ref[:, hh * MLA_V:(hh + 1) * MLA_V] = (acc / l).T.astype(BF16)

    for r in range(nq):
        @pl.when(n % nq == r)
        def _(r=r):
            scores(r, r % 2)
            attend((r - 1) % nq, (r - 1) % 2)


def _mla_attn(q, kv, kpe, b, s, hp=2, tq=512):
    assert MLA_NOPE == MLA_V
    t = b * s
    nq = s // tq
    assert nq % 2 == 0
    groups = MLA_HEADS // hp
    units = b * groups * nq
    scale = (MLA_NOPE + MLA_ROPE) ** -0.5

    def unit(n):
        return n // (groups * nq), (n // nq) % groups, n % nq

    def cur(n):
        return unit(jnp.minimum(n, units - 1))

    def prev(n):
        return unit(jnp.maximum(n - 1, 0))

    def q_map(n):
        bb, g, i = cur(n)
        return bb * nq + i, g

    def k_map(n):
        bb, g, _ = cur(n)
        return bb, g

    def kpe_map(n):
        return cur(n)[0], 0

    def v_map(n):
        bb, g, _ = prev(n)
        return bb, groups + g

    def o_map(n):
        bb, g, i = prev(n)
        return bb * nq + i, g

    return pl.pallas_call(
        functools.partial(_mla_attn_kernel, hp=hp, tq=tq, nq=nq, scale=scale),
        grid=(units + 1,),
        in_specs=[pl.BlockSpec((tq, hp * MLA_QW), q_map),
                  pl.BlockSpec((s, hp * MLA_NOPE), k_map),
                  pl.BlockSpec((s, LANES), kpe_map),
                  pl.BlockSpec((s, hp * MLA_V), v_map)],
        out_specs=pl.BlockSpec((tq, hp * MLA_V), o_map),
        out_shape=jax.ShapeDtypeStruct((t, MLA_HEADS * MLA_V), BF16),
        scratch_shapes=[pltpu.VMEM((2, hp, s, tq), F32),
                        pltpu.VMEM((2, hp, 1, tq), F32),
                        pltpu.VMEM((hp, s, tq), BF16)],
        compiler_params=_params(("arbitrary",)),
        name="mla_attn",
    )(q, kv, kpe, kv)


def _t5_bucket_np(n):
    max_exact = REL_BUCKETS // 2
    nf = np.maximum(n, max_exact).astype(np.float32)
    large = max_exact + (np.log(nf / np.float32(max_exact))
                         / np.float32(math.log(REL_MAX_DIST / max_exact))
                         * np.float32(REL_BUCKETS - max_exact)).astype(np.int32)
    large = np.minimum(large, REL_BUCKETS - 1)
    return np.where(n < max_exact, n, large).astype(np.int32)


def _moba_bias_kernel(tab_ref, bucket_ref, o_ref):
    h = pl.program_id(0)
    bucket = bucket_ref[...]
    acc = jnp.zeros(bucket.shape, F32)
    for k in range(REL_BUCKETS):
        acc = jnp.where(bucket == k, tab_ref[k, h], acc)
    o_ref[0] = acc


def _moba_bias(rel_bias):
    blk = MOBA_BLOCK
    key = np.arange(blk)[:, None]
    qry = np.arange(blk)[None, :]
    dist = np.stack([np.maximum(qry - key, 0), blk + qry - key])
    bucket = jnp.asarray(_t5_bucket_np(dist))
    return pl.pallas_call(
        _moba_bias_kernel,
        grid=(MOBA_HEADS,),
        in_specs=[pl.BlockSpec(memory_space=pltpu.SMEM),
                  pl.BlockSpec((2, blk, blk), lambda h: (0, 0, 0))],
        out_specs=pl.BlockSpec((1, 2, blk, blk), lambda h: (h, 0, 0, 0)),
        out_shape=jax.ShapeDtypeStruct((MOBA_HEADS, 2, blk, blk), F32),
        compiler_params=_params(("arbitrary",)),
        name="moba_bias",
    )(rel_bias, bucket)


def _moba_attn_kernel(tab_ref, q_ref, k_ref, v_ref, bias_ref, o_ref, sel_ref, *, hp, nb, scale):
    g = pl.program_id(1)
    i = pl.program_id(2)
    blk = MOBA_BLOCK
    dh = MOBA_HEAD_DIM
    s_len = nb * blk
    q0 = pl.multiple_of(i * blk, blk)

    @pl.when(i == 0)
    def _():
        nidx = lax.broadcasted_iota(jnp.int32, (nb, s_len), 0)
        qblk = lax.broadcasted_iota(jnp.int32, (nb, s_len), 1) // blk
        past = nidx < qblk
        for hh in range(hp):
            cs = slice(hh * dh, (hh + 1) * dh)
            k_mean = jnp.concatenate(
                [jnp.sum(k_ref[n * blk:(n + 1) * blk, cs].astype(F32), axis=0, keepdims=True)
                 for n in range(nb)], axis=0) / blk
            gate = lax.dot_general(k_mean, q_ref[:, cs].astype(F32), NT_DIMS,
                                   precision=HIGHEST, preferred_element_type=F32)
            gm = jnp.where(past, gate, NEG_INF)
            rank = jnp.zeros((nb, s_len), F32)
            for m in range(nb):
                gm_m = gm[m:m + 1, :]
                tie = jnp.where(gm_m == gm, jnp.where(nidx > m, 1.0, 0.0), 0.0)
                rank = rank + jnp.where(gm_m > gm, 1.0, tie)
            sel = jnp.where(rank < min(MOBA_TOPK, nb), jnp.where(past, 1.0, 0.0), 0.0)
            for qb in range(nb):
                sel_ref[hh, qb] = sel[:, qb * blk:(qb + 1) * blk]

    kidx = lax.broadcasted_iota(jnp.int32, (blk, blk), 0)
    qidx = lax.broadcasted_iota(jnp.int32, (blk, blk), 1)
    carry = []
    for hh in range(hp):
        cs = slice(hh * dh, (hh + 1) * dh)
        q = q_ref[pl.ds(q0, blk), cs]
        s = lax.dot_general(k_ref[pl.ds(q0, blk), cs], q, NT_DIMS,
                            preferred_element_type=F32) * scale + bias_ref[hh, 0]
        s = jnp.where(kidx <= qidx, s, NEG_INF)
        carry.append(_softmax_step(
            s, jnp.full((1, blk), NEG_INF, F32), jnp.zeros((1, blk), F32),
            jnp.zeros((dh, blk), F32), v_ref[pl.ds(q0, blk), cs]))

    def past_block(j, carry):
        k0 = pl.multiple_of(j * blk, blk)
        adjacent = (i - j) == 1
        out = []
        for hh in range(hp):
            m, l, acc = carry[hh]
            cs = slice(hh * dh, (hh + 1) * dh)
            q = q_ref[pl.ds(q0, blk), cs]
            far_bias = tab_ref[REL_BUCKETS - 1, g * hp + hh]
            bias = jnp.where(adjacent, bias_ref[hh, 1], far_bias)
            s = lax.dot_general(k_ref[pl.ds(k0, blk), cs], q, NT_DIMS,
                                preferred_element_type=F32) * scale + bias
            sel = sel_ref[hh, i, pl.ds(j, 1), :]
            s = jnp.where(sel > 0.5, s, NEG_INF)
            out.append(_softmax_step(s, m, l, acc, v_ref[pl.ds(k0, blk), cs]))
        return tuple(out)

    carry = lax.fori_loop(0, i, past_block, tuple(carry))
    for hh in range(hp):
        _, l, acc = carry[hh]
        o_ref[:, hh * dh:(hh + 1) * dh] = (acc / l).T.astype(BF16)


def _moba_attn(qkv, bias, rel_bias, b, s, hp=4):
    assert MOBA_BLOCK + 1 >= REL_MAX_DIST and s % MOBA_BLOCK == 0
    t = b * s
    blk = MOBA_BLOCK
    nb = s // blk
    dh = MOBA_HEAD_DIM
    groups = MOBA_HEADS // hp
    return pl.pallas_call(
        functools.partial(_moba_attn_kernel, hp=hp, nb=nb, scale=dh ** -0.5),
        grid=(b, groups, nb),
        in_specs=[pl.BlockSpec(memory_space=pltpu.SMEM),
                  pl.BlockSpec((s, hp * dh), lambda bb, g, i: (bb, g)),
                  pl.BlockSpec((s, hp * dh), lambda bb, g, i: (bb, groups + g)),
                  pl.BlockSpec((s, hp * dh), lambda bb, g, i: (bb, 2 * groups + g)),
                  pl.BlockSpec((hp, 2, blk, blk), lambda bb, g, i: (g, 0, 0, 0))],
        out_specs=pl.BlockSpec((blk, hp * dh), lambda bb, g, i: (bb * nb + i, g)),
        out_shape=jax.ShapeDtypeStruct((t, MOBA_HEADS * dh), BF16),
        scratch_shapes=[pltpu.VMEM((hp, nb, nb, blk), F32)],
        compiler_params=_params(("parallel", "parallel", "arbitrary")),
        name="moba_attn",
    )(rel_bias, qkv, qkv, qkv, bias)


SSD_GN = SSD_GROUPS * SSD_STATE
SSD_GW = SSD_INNER // SSD_GROUPS


def _softplus(x):
    return jnp.maximum(x, 0.0) + jnp.log1p(jnp.exp(-jnp.abs(x)))


def _silu(x):
    return x * jax.nn.sigmoid(x)


def _expand_heads(v, e3):
    h1 = v.astype(BF16)
    r1 = v - h1.astype(F32)
    h2 = r1.astype(BF16)
    h3 = (r1 - h2.astype(F32)).astype(BF16)
    return jnp.dot(jnp.concatenate([h1, h2, h3], axis=1), e3, preferred_element_type=F32)


def _ssd_kernel(z_ref, xbc_ref, dt_ref, cw_ref, cb_ref, dtb_ref, alog_ref, dexp_ref,
                nw_ref, e3_ref, o_ref, xpad_ref, st_ref):
    c = pl.program_id(1)
    L = SSD_CHUNK
    P = SSD_HEAD_DIM
    N = SSD_STATE
    halo = SUBLANES

    @pl.when(c == 0)
    def _():
        xpad_ref[0:halo, :] = jnp.zeros((halo, SSD_CONV_DIM), F32)
        st_ref[...] = jnp.zeros_like(st_ref)

    xpad_ref[halo:halo + L, :] = xbc_ref[...]
    conv = cb_ref[...] + cw_ref[SSD_CONV - 1:SSD_CONV, :] * xbc_ref[...]
    for k in range(SSD_CONV - 1):
        r0 = halo - (SSD_CONV - 1) + k
        conv = conv + cw_ref[k:k + 1, :] * xpad_ref[r0:r0 + L, :]
    xpad_ref[0:halo, :] = xpad_ref[L:L + halo, :]
    xc = _silu(conv)
    xs = xc[:, :SSD_INNER]
    bm = xc[:, SSD_INNER:SSD_INNER + SSD_GN].astype(BF16)
    cm = xc[:, SSD_INNER + SSD_GN:].astype(BF16)
    xs_b = xs.astype(BF16)

    dt = _softplus(dt_ref[...] + dtb_ref[...])
    a = -jnp.exp(alog_ref[...])
    row = lax.broadcasted_iota(jnp.int32, (L, L), 0)
    col = lax.broadcasted_iota(jnp.int32, (L, L), 1)
    tril = row >= col
    a_cs = jnp.dot(jnp.where(tril, 1.0, 0.0).astype(F32), dt * a,
                   precision=HIGHEST, preferred_element_type=F32)
    a_cs_t = a_cs.T
    dt_t = dt.T
    e3 = e3_ref[...]
    decay_exp = _expand_heads(jnp.exp(a_cs), e3)
    w_exp = _expand_heads(jnp.exp(a_cs[L - 1:L, :] - a_cs) * dt, e3)
    xw = (xs * w_exp).astype(BF16)
    lane = lax.broadcasted_iota(jnp.int32, (L, LANES), 1)

    ys = []
    for g in range(SSD_GROUPS):
        gs = slice(g * SSD_GW, (g + 1) * SSD_GW)
        b_g = bm[:, g * N:(g + 1) * N]
        c_g = cm[:, g * N:(g + 1) * N]
        cb = lax.dot_general(c_g, b_g, NT_DIMS, preferred_element_type=F32)
        state = st_ref[:, gs]
        y_off = jnp.dot(c_g, state.astype(BF16), preferred_element_type=F32) * decay_exp[:, gs]
        new_state = lax.dot_general(b_g, xw[:, gs], TN_DIMS, preferred_element_type=F32)
        st_ref[:, gs] = state * decay_exp[L - 1:L, gs] + new_state
        y_diag = []
        for pr in range(SSD_HEADS_PER_GROUP // 2):
            h0 = g * SSD_HEADS_PER_GROUP + 2 * pr
            mats = []
            for hr in (h0, h0 + 1):
                seg = a_cs[:, hr:hr + 1] - a_cs_t[hr:hr + 1, :]
                decay = jnp.exp(jnp.where(tril, seg, NEG_INF))
                mats.append((cb * decay * dt_t[hr:hr + 1, :]).astype(BF16))
            xp = xs_b[:, h0 * P:(h0 + 2) * P]
            zero = jnp.zeros_like(xp)
            x2 = jnp.concatenate([jnp.where(lane < P, xp, zero),
                                  jnp.where(lane >= P, xp, zero)], axis=0)
            y_diag.append(jnp.dot(jnp.concatenate(mats, axis=1), x2,
                                  preferred_element_type=F32))
        ys.append(jnp.concatenate(y_diag, axis=1) + y_off)
    y = jnp.concatenate(ys, axis=1) + dexp_ref[...] * xs
    gated = y * _silu(z_ref[...])
    outs = []
    for g in range(SSD_GROUPS):
        gg = gated[:, g * SSD_GW:(g + 1) * SSD_GW]
        ms = jnp.mean(gg * gg, axis=-1, keepdims=True)
        outs.append(gg * lax.rsqrt(ms + NORM_EPS))
    o_ref[...] = (jnp.concatenate(outs, axis=1) * nw_ref[...]).astype(BF16)


def _ssd_scan(z, xbc, dt, conv_w, conv_b, dt_bias, a_log, d_skip, norm_w, b, s):
    t = b * s
    L = SSD_CHUNK
    nc = s // L
    pad_h = lambda v: jnp.pad(v, (0, LANES - SSD_HEADS)).reshape(1, LANES)
    e = np.zeros((LANES, SSD_INNER), np.float32)
    for r in range(SSD_HEADS):
        e[r, r * SSD_HEAD_DIM:(r + 1) * SSD_HEAD_DIM] = 1.0
    e3 = jnp.asarray(np.concatenate([e, e, e], axis=0), dtype=BF16)
    d_exp = jnp.repeat(d_skip, SSD_HEAD_DIM).reshape(1, SSD_INNER)
    row = lambda bb, c: (bb * nc + c, 0)
    fixed = lambda bb, c: (0, 0)
    return pl.pallas_call(
        _ssd_kernel,
        grid=(b, nc),
        in_specs=[pl.BlockSpec((L, SSD_INNER), row),
                  pl.BlockSpec((L, SSD_CONV_DIM), row),
                  pl.BlockSpec((L, LANES), row),
                  pl.BlockSpec((SSD_CONV, SSD_CONV_DIM), fixed),
                  pl.BlockSpec((1, SSD_CONV_DIM), fixed),
                  pl.BlockSpec((1, LANES), fixed),
                  pl.BlockSpec((1, LANES), fixed),
                  pl.BlockSpec((1, SSD_INNER), fixed),
                  pl.BlockSpec((1, SSD_INNER), fixed),
                  pl.BlockSpec((3 * LANES, SSD_INNER), fixed)],
        out_specs=pl.BlockSpec((L, SSD_INNER), row),
        out_shape=jax.ShapeDtypeStruct((t, SSD_INNER), BF16),
        scratch_shapes=[pltpu.VMEM((L + SUBLANES, SSD_CONV_DIM), F32),
                        pltpu.VMEM((SSD_STATE, SSD_INNER), F32)],
        compiler_params=_params(("parallel", "arbitrary")),
        name="ssd_scan",
    )(z, xbc, dt, conv_w, conv_b.reshape(1, -1), pad_h(dt_bias), pad_h(a_log), d_exp,
      norm_w.reshape(1, -1), e3)


def _post_kernel(x_ref, a_ref, wo_ref, g_ref, w1_ref, w2_ref, gf_ref, o_ref,
                 x1_ref, h_ref, acc_ref, *, final_norm):
    j = pl.program_id(1)

    @pl.when(j == 0)
    def _():
        x1 = x_ref[...] + jnp.dot(a_ref[...], wo_ref[...], preferred_element_type=F32)
        x1_ref[...] = x1
        h_ref[...] = _rms(x1, g_ref[...]).astype(BF16)
        acc_ref[...] = jnp.zeros_like(acc_ref)

    u = jnp.maximum(jnp.dot(h_ref[...], w1_ref[...], preferred_element_type=F32), 0.0)
    acc_ref[...] += jnp.dot((u * u).astype(BF16), w2_ref[...], preferred_element_type=F32)

    @pl.when(j == pl.num_programs(1) - 1)
    def _():
        y = x1_ref[...] + acc_ref[...]
        if final_norm:
            y = _rms(y, gf_ref[...])
        o_ref[...] = y


def _post(x, a, w_o, g, w1, w2, g_final, final_norm, tm=512, th=1024):
    t, d = x.shape
    ka = a.shape[1]
    hid = w1.shape[1]
    return pl.pallas_call(
        functools.partial(_post_kernel, final_norm=final_norm),
        grid=(t // tm, hid // th),
        in_specs=[pl.BlockSpec((tm, d), lambda i, j: (i, 0)),
                  pl.BlockSpec((tm, ka), lambda i, j: (i, 0)),
                  pl.BlockSpec((ka, d), lambda i, j: (0, 0)),
                  pl.BlockSpec((1, d), lambda i, j: (0, 0)),
                  pl.BlockSpec((d, th), lambda i, j: (0, j)),
                  pl.BlockSpec((th, d), lambda i, j: (j, 0)),
                  pl.BlockSpec((1, d), lambda i, j: (0, 0))],
        out_specs=pl.BlockSpec((tm, d), lambda i, j: (i, 0)),
        out_shape=jax.ShapeDtypeStruct((t, d), F32),
        scratch_shapes=[pltpu.VMEM((tm, d), F32), pltpu.VMEM((tm, d), BF16),
                        pltpu.VMEM((tm, d), F32)],
        compiler_params=_params(("parallel", "arbitrary"), 56),
        name="post_mlp",
    )(x, a, w_o.astype(BF16), g.reshape(1, d), w1.astype(BF16), w2.astype(BF16),
      g_final.reshape(1, d))


def _mla_mixer(x, g, rope, w_in, q_norm, w_uq, kv_norm, w_ukv, b, s):
    q, kv, kpe = _mla_proj(x, g, w_in, q_norm, w_uq, kv_norm, w_ukv, rope)
    return _mla_attn(q, kv, kpe, b, s)


def _moba_mixer(x, g, w_qkv, bias, rel_bias, b, s):
    n = w_qkv.shape[1]
    (qkv,) = _norm_proj(x, g, w_qkv.astype(BF16), (n,), (BF16,), tm=512)
    return _moba_attn(qkv, bias, rel_bias, b, s)


def _ssd_mixer(x, g, w_in, conv_w, conv_b, dt_bias, a_log, d_skip, norm_w, b, s):
    w_pad = jnp.pad(w_in, ((0, 0), (0, LANES - SSD_HEADS))).astype(BF16)
    z, xbc, dt = _norm_proj(x, g, w_pad, (SSD_INNER, SSD_CONV_DIM, LANES),
                            (F32, F32, F32), tm=256)
    return _ssd_scan(z, xbc, dt, conv_w, conv_b, dt_bias, a_log, d_skip, norm_w, b, s)


def kernel(x, positions, norm_mix, norm_mlp, norm_final, rel_bias, mla_w_in, mla_q_norm, mla_w_uq, mla_kv_norm, mla_w_ukv, mla_w_o, moba_w_qkv, moba_w_o, ssd_w_in, ssd_conv_w, ssd_conv_b, ssd_dt_bias, ssd_a_log, ssd_d, ssd_norm, ssd_w_out, mlp_w1, mlp_w2):
    b, s, d = x.shape
    xf = x.reshape(b * s, d)
    rope = None
    bias = None
    for i in range(DEPTH):
        kind, j = i % N_MIXERS, i // N_MIXERS
        if kind == 0:
            if rope is None:
                rope = _rope_tables(positions)
            a = _mla_mixer(xf, norm_mix[i], rope, mla_w_in[j], mla_q_norm[j], mla_w_uq[j],
                           mla_kv_norm[j], mla_w_ukv[j], b, s)
            w_o = mla_w_o[j]
        elif kind == 1:
            if bias is None:
                bias = _moba_bias(rel_bias)
            a = _moba_mixer(xf, norm_mix[i], moba_w_qkv[j], bias, rel_bias, b, s)
            w_o = moba_w_o[j]
        else:
            a = _ssd_mixer(xf, norm_mix[i], ssd_w_in[j], ssd_conv_w[j], ssd_conv_b[j],
                           ssd_dt_bias[j], ssd_a_log[j], ssd_d[j], ssd_norm[j], b, s)
            w_o = ssd_w_out[j]
        xf = _post(xf, a, w_o, norm_mlp[i], mlp_w1[i], mlp_w2[i], norm_final,
                   final_norm=(i == DEPTH - 1))
    return xf.reshape(b, s, d)
```

```python
import functools
import math

import numpy as np
import jax
import jax.numpy as jnp
from jax import lax
from jax.experimental import pallas as pl
from jax.experimental.pallas import tpu as pltpu

F32 = jnp.float32
BF16 = jnp.bfloat16
HIGHEST = lax.Precision.HIGHEST

D_MODEL = 1024
DEPTH = 4
N_MIXERS = 3
NORM_EPS = 1e-6
NEG_INF = -1e30

MLA_HEADS = 16
MLA_Q_RANK = 256
MLA_KV_RANK = 256
MLA_NOPE = 128
MLA_ROPE = 64
MLA_V = 128
ROPE_BASE = 10000.0

MOBA_HEADS = 8
MOBA_HEAD_DIM = D_MODEL // MOBA_HEADS
MOBA_BLOCK = 256
MOBA_TOPK = 3
REL_BUCKETS = 32
REL_MAX_DIST = 128

SSD_INNER = 2 * D_MODEL
SSD_HEAD_DIM = 64
SSD_HEADS = SSD_INNER // SSD_HEAD_DIM
SSD_GROUPS = 2
SSD_HEADS_PER_GROUP = SSD_HEADS // SSD_GROUPS
SSD_STATE = 128
SSD_CONV = 4
SSD_CONV_DIM = SSD_INNER + 2 * SSD_GROUPS * SSD_STATE
SSD_CHUNK = 128

MLP_HIDDEN = 4 * D_MODEL

LANES = 128
SUBLANES = 8
MIB = 1024 * 1024

NT_DIMS = (((1,), (1,)), ((), ()))
TN_DIMS = (((0,), (0,)), ((), ()))


def _params(semantics, vmem_mib=48):
    return pltpu.CompilerParams(dimension_semantics=semantics,
                                vmem_limit_bytes=vmem_mib * MIB)


def _rms(x, g):
    ms = jnp.mean(x * x, axis=-1, keepdims=True)
    return x * lax.rsqrt(ms + NORM_EPS) * g


def _norm_proj_kernel(x_ref, g_ref, w_ref, *o_refs, splits, col_chunk):
    h = _rms(x_ref[...], g_ref[...]).astype(BF16)
    off = 0
    for o_ref, n in zip(o_refs, splits):
        for c0 in range(0, n, col_chunk):
            c1 = min(c0 + col_chunk, n)
            o_ref[:, c0:c1] = jnp.dot(
                h, w_ref[:, off + c0:off + c1],
                preferred_element_type=F32).astype(o_ref.dtype)
        off += n


def _norm_proj(x, g, w, splits, dtypes, tm):
    t, d = x.shape
    n = w.shape[1]
    assert sum(splits) == n and t % tm == 0
    return pl.pallas_call(
        functools.partial(_norm_proj_kernel, splits=tuple(splits), col_chunk=1024),
        grid=(t // tm,),
        in_specs=[pl.BlockSpec((tm, d), lambda i: (i, 0)),
                  pl.BlockSpec((1, d), lambda i: (0, 0)),
                  pl.BlockSpec((d, n), lambda i: (0, 0))],
        out_specs=[pl.BlockSpec((tm, s), lambda i: (i, 0)) for s in splits],
        out_shape=[jax.ShapeDtypeStruct((t, s), dt) for s, dt in zip(splits, dtypes)],
        compiler_params=_params(("parallel",), 56),
        name="norm_proj",
    )(x, g.reshape(1, d), w)


def _rope_table_kernel(pos_ref, freq_ref, cos_ref, sin_ref):
    ang = pos_ref[...].astype(F32) * freq_ref[...]
    cos_ref[...] = jnp.cos(ang)
    sin_ref[...] = jnp.sin(ang)


def _rope_tables(positions):
    b, s = positions.shape
    t = b * s
    half = MLA_ROPE // 2
    per_row = LANES // half
    rows = t // per_row
    inv_freq = ROPE_BASE ** (-(jnp.arange(0, MLA_ROPE, 2, dtype=F32) / MLA_ROPE))
    pos_rep = jnp.repeat(positions.reshape(rows, per_row), half, axis=1)
    freq = jnp.tile(inv_freq, per_row).reshape(1, LANES)
    tr = 512
    cos, sin = pl.pallas_call(
        _rope_table_kernel,
        grid=(rows // tr,),
        in_specs=[pl.BlockSpec((tr, LANES), lambda i: (i, 0)),
                  pl.BlockSpec((1, LANES), lambda i: (0, 0))],
        out_specs=[pl.BlockSpec((tr, LANES), lambda i: (i, 0))] * 2,
        out_shape=[jax.ShapeDtypeStruct((rows, LANES), F32)] * 2,
        compiler_params=_params(("parallel",)),
        name="rope_tables",
    )(pos_rep, freq)
    cos = cos.reshape(t, half)
    sin = sin.reshape(t, half)
    zeros = lambda n: jnp.zeros((t, n), F32)
    cos_t = jnp.concatenate([cos, cos, zeros(LANES - 2 * half)], axis=1)
    sin_a = jnp.concatenate([-sin, zeros(LANES - half)], axis=1)
    sin_b = jnp.concatenate([zeros(half), sin, zeros(LANES - 2 * half)], axis=1)
    return cos_t, sin_a, sin_b


def _rope128(v, cos_t, sin_a, sin_b):
    half = MLA_ROPE // 2
    return (v * cos_t + pltpu.roll(v, LANES - half, 1) * sin_a
            + pltpu.roll(v, half, 1) * sin_b)


MLA_QW = 2 * LANES
MLA_KVW = MLA_NOPE + MLA_V
MLA_IN_PAD = MLA_Q_RANK + MLA_KV_RANK + LANES


def _mla_proj_kernel(x_ref, g_ref, win_ref, qn_ref, kvn_ref, wuq_ref, wukv_ref,
                     cos_ref, sa_ref, sb_ref, q_ref, kv_ref, kpe_ref):
    h = _rms(x_ref[...], g_ref[...]).astype(BF16)
    proj = jnp.dot(h, win_ref[...], preferred_element_type=F32)
    cq = _rms(proj[:, :MLA_Q_RANK], qn_ref[...]).astype(BF16)
    ckv = _rms(proj[:, MLA_Q_RANK:MLA_Q_RANK + MLA_KV_RANK], kvn_ref[...]).astype(BF16)
    cos_t, sin_a, sin_b = cos_ref[...], sa_ref[...], sb_ref[...]
    kpe_ref[...] = _rope128(proj[:, MLA_Q_RANK + MLA_KV_RANK:], cos_t, sin_a, sin_b).astype(BF16)
    for hh in range(MLA_HEADS):
        c0 = hh * MLA_QW
        qh = jnp.dot(cq, wuq_ref[:, c0:c0 + MLA_QW], preferred_element_type=F32)
        q_ref[:, c0:c0 + MLA_NOPE] = qh[:, :MLA_NOPE].astype(BF16)
        q_ref[:, c0 + MLA_NOPE:c0 + MLA_QW] = _rope128(
            qh[:, MLA_NOPE:], cos_t, sin_a, sin_b).astype(BF16)
    n_kv = MLA_HEADS * MLA_KVW
    for c0 in range(0, n_kv, 1024):
        kv_ref[:, c0:c0 + 1024] = jnp.dot(
            ckv, wukv_ref[:, c0:c0 + 1024], preferred_element_type=F32).astype(BF16)


def _mla_proj(x, g, w_in, q_norm, w_uq, kv_norm, w_ukv, rope, tm=256):
    t, d = x.shape
    nq = MLA_HEADS * MLA_QW
    nkv = MLA_HEADS * MLA_KVW
    w_in_p = jnp.pad(w_in, ((0, 0), (0, MLA_IN_PAD - w_in.shape[1]))).astype(BF16)
    w_uq_p = jnp.pad(w_uq.reshape(MLA_Q_RANK, MLA_HEADS, MLA_NOPE + MLA_ROPE),
                     ((0, 0), (0, 0), (0, MLA_QW - MLA_NOPE - MLA_ROPE))
                     ).reshape(MLA_Q_RANK, nq).astype(BF16)
    w_ukv_p = jnp.swapaxes(w_ukv.reshape(MLA_KV_RANK, MLA_HEADS, 2, MLA_NOPE), 1, 2
                           ).reshape(MLA_KV_RANK, nkv).astype(BF16)
    row = lambda i: (i, 0)
    fixed = lambda i: (0, 0)
    return pl.pallas_call(
        _mla_proj_kernel,
        grid=(t // tm,),
        in_specs=[pl.BlockSpec((tm, d), row),
                  pl.BlockSpec((1, d), fixed),
                  pl.BlockSpec((d, MLA_IN_PAD), fixed),
                  pl.BlockSpec((1, MLA_Q_RANK), fixed),
                  pl.BlockSpec((1, MLA_KV_RANK), fixed),
                  pl.BlockSpec((MLA_Q_RANK, nq), fixed),
                  pl.BlockSpec((MLA_KV_RANK, nkv), fixed),
                  pl.BlockSpec((tm, LANES), row),
                  pl.BlockSpec((tm, LANES), row),
                  pl.BlockSpec((tm, LANES), row)],
        out_specs=[pl.BlockSpec((tm, nq), row),
                   pl.BlockSpec((tm, nkv), row),
                   pl.BlockSpec((tm, LANES), row)],
        out_shape=[jax.ShapeDtypeStruct((t, nq), BF16),
                   jax.ShapeDtypeStruct((t, nkv), BF16),
                   jax.ShapeDtypeStruct((t, LANES), BF16)],
        compiler_params=_params(("parallel",)),
        name="mla_proj",
    )(x, g.reshape(1, d), w_in_p, q_norm.reshape(1, -1), kv_norm.reshape(1, -1),
      w_uq_p, w_ukv_p, *rope)


LOG2E = math.log2(math.e)


def _softmax_step(s, m, l, acc, v, scale=1.0):
    c = scale * LOG2E
    m_new = jnp.maximum(m, jnp.max(s, axis=0, keepdims=True))
    alpha = jnp.exp2((m - m_new) * c)
    p = jnp.exp2((s - m_new) * c)
    l_new = alpha * l + jnp.sum(p, axis=0, keepdims=True)
    pv = lax.dot_general(v, p.astype(BF16), TN_DIMS, preferred_element_type=F32)
    return m_new, l_new, alpha * acc + pv


def _mla_attn_kernel(q_ref, k_ref, kpe_ref, v_ref, o_ref, s_ref, m_ref, p_ref,
                     *, hp, tq, nq, scale):
    n = pl.program_id(0)
    c = scale * LOG2E
    kidx = lax.broadcasted_iota(jnp.int32, (tq, tq), 0)
    qidx = lax.broadcasted_iota(jnp.int32, (tq, tq), 1)
    causal = kidx <= qidx

    @pl.when(n == 0)
    def _():
        s_ref[...] = jnp.zeros_like(s_ref)
        m_ref[...] = jnp.zeros_like(m_ref)

    def scores(qi, slot):
        past = qi * tq
        kpe = kpe_ref[0:past + tq, :]
        for hh in range(hp):
            kc = jnp.concatenate(
                [k_ref[0:past + tq, hh * MLA_NOPE:(hh + 1) * MLA_NOPE], kpe], axis=1)
            q = q_ref[:, hh * MLA_QW:(hh + 1) * MLA_QW]
            s_diag = jnp.where(
                causal, lax.dot_general(kc[past:], q, NT_DIMS, preferred_element_type=F32),
                NEG_INF)
            s_ref[slot, hh, past:past + tq, :] = s_diag
            m = jnp.max(s_diag, axis=0, keepdims=True)
            if past:
                s_past = lax.dot_general(kc[:past], q, NT_DIMS, preferred_element_type=F32)
                s_ref[slot, hh, 0:past, :] = s_past
                m = jnp.maximum(m, jnp.max(s_past, axis=0, keepdims=True))
            m_ref[slot, hh] = m

    def attend(qi, slot):
        kv_len = (qi + 1) * tq
        for hh in range(hp):
            p = jnp.exp2((s_ref[slot, hh, 0:kv_len, :] - m_ref[slot, hh]) * c)
            l = jnp.sum(p, axis=0, keepdims=True)
            p_ref[hh, 0:kv_len, :] = p.astype(BF16)
            acc = lax.dot_general(v_ref[0:kv_len, hh * MLA_V:(hh + 1) * MLA_V],
                                  p_ref[hh, 0:kv_len, :], TN_DIMS, preferred_element_type=F32)
            o_ref[:, hh * MLA_V:(hh + 1) * MLA_V] = (acc / l).T.astype(BF16)

    for r in range(nq):
        @pl.when(n % nq == r)
        def _(r=r):
            scores(r, r % 2)
            attend((r - 1) % nq, (r - 1) % 2)


def _mla_attn(q, kv, kpe, b, s, hp=2, tq=512):
    assert MLA_NOPE == MLA_V
    t = b * s
    nq = s // tq
    assert nq % 2 == 0
    groups = MLA_HEADS // hp
    units = b * groups * nq
    scale = (MLA_NOPE + MLA_ROPE) ** -0.5

    def unit(n):
        return n // (groups * nq), (n // nq) % groups, n % nq

    def cur(n):
        return unit(jnp.minimum(n, units - 1))

    def prev(n):
        return unit(jnp.maximum(n - 1, 0))

    def q_map(n):
        bb, g, i = cur(n)
        return bb * nq + i, g

    def k_map(n):
        bb, g, _ = cur(n)
        return bb, g

    def kpe_map(n):
        return cur(n)[0], 0

    def v_map(n):
        bb, g, _ = prev(n)
        return bb, groups + g

    def o_map(n):
        bb, g, i = prev(n)
        return bb * nq + i, g

    return pl.pallas_call(
        functools.partial(_mla_attn_kernel, hp=hp, tq=tq, nq=nq, scale=scale),
        grid=(units + 1,),
        in_specs=[pl.BlockSpec((tq, hp * MLA_QW), q_map),
                  pl.BlockSpec((s, hp * MLA_NOPE), k_map),
                  pl.BlockSpec((s, LANES), kpe_map),
                  pl.BlockSpec((s, hp * MLA_V), v_map)],
        out_specs=pl.BlockSpec((tq, hp * MLA_V), o_map),
        out_shape=jax.ShapeDtypeStruct((t, MLA_HEADS * MLA_V), BF16),
        scratch_shapes=[pltpu.VMEM((2, hp, s, tq), F32),
                        pltpu.VMEM((2, hp, 1, tq), F32),
                        pltpu.VMEM((hp, s, tq), BF16)],
        compiler_params=_params(("arbitrary",)),
        name="mla_attn",
    )(q, kv, kpe, kv)


def _t5_bucket_np(n):
    max_exact = REL_BUCKETS // 2
    nf = np.maximum(n, max_exact).astype(np.float32)
    large = max_exact + (np.log(nf / np.float32(max_exact))
                         / np.float32(math.log(REL_MAX_DIST / max_exact))
                         * np.float32(REL_BUCKETS - max_exact)).astype(np.int32)
    large = np.minimum(large, REL_BUCKETS - 1)
    return np.where(n < max_exact, n, large).astype(np.int32)


def _moba_bias_kernel(tab_ref, bucket_ref, o_ref, *, inv_scale):
    h = pl.program_id(0)
    bucket = bucket_ref[...]
    far = tab_ref[REL_BUCKETS - 1, h]
    acc = jnp.zeros(bucket.shape, F32)
    for k in range(REL_BUCKETS - 1):
        acc = jnp.where(bucket == k, tab_ref[k, h] - far, acc)
    o_ref[0] = acc * inv_scale


def _moba_bias(rel_bias):
    blk = MOBA_BLOCK
    key = np.arange(blk)[:, None]
    qry = np.arange(blk)[None, :]
    dist = np.stack([np.maximum(qry - key, 0), blk + qry - key])
    bucket = jnp.asarray(_t5_bucket_np(dist))
    return pl.pallas_call(
        functools.partial(_moba_bias_kernel, inv_scale=MOBA_HEAD_DIM ** 0.5),
        grid=(MOBA_HEADS,),
        in_specs=[pl.BlockSpec(memory_space=pltpu.SMEM),
                  pl.BlockSpec((2, blk, blk), lambda h: (0, 0, 0))],
        out_specs=pl.BlockSpec((1, 2, blk, blk), lambda h: (h, 0, 0, 0)),
        out_shape=jax.ShapeDtypeStruct((MOBA_HEADS, 2, blk, blk), F32),
        compiler_params=_params(("arbitrary",)),
        name="moba_bias",
    )(rel_bias, bucket)


def _moba_attn_kernel(q_ref, k_ref, v_ref, bias_ref, o_ref,
                      sel_ref, s_ref, m_ref, p_ref, *, hp, nb, scale):
    n = pl.program_id(0)
    blk = MOBA_BLOCK
    dh = MOBA_HEAD_DIM
    s_len = nb * blk
    kidx = lax.broadcasted_iota(jnp.int32, (blk, blk), 0)
    qidx = lax.broadcasted_iota(jnp.int32, (blk, blk), 1)
    causal = kidx <= qidx

    @pl.when(n == 0)
    def _():
        s_ref[...] = jnp.zeros_like(s_ref)
        m_ref[...] = jnp.zeros_like(m_ref)

    def select_blocks():
        nidx = lax.broadcasted_iota(jnp.int32, (nb, s_len), 0)
        qblk = lax.broadcasted_iota(jnp.int32, (nb, s_len), 1) // blk
        past = nidx < qblk
        for hh in range(hp):
            cs = slice(hh * dh, (hh + 1) * dh)
            k_mean = jnp.concatenate(
                [jnp.sum(k_ref[n * blk:(n + 1) * blk, cs].astype(F32), axis=0, keepdims=True)
                 for n in range(nb)], axis=0) / blk
            h1 = k_mean.astype(BF16)
            r1 = k_mean - h1.astype(F32)
            h2 = r1.astype(BF16)
            h3 = (r1 - h2.astype(F32)).astype(BF16)
            g3 = lax.dot_general(jnp.concatenate([h1, h2, h3], axis=0), q_ref[:, cs], NT_DIMS,
                                 preferred_element_type=F32)
            gate = g3[0:nb] + g3[nb:2 * nb] + g3[2 * nb:3 * nb]
            gm = jnp.where(past, gate, NEG_INF)
            rank = jnp.zeros((nb, s_len), F32)
            for m in range(nb):
                gm_m = gm[m:m + 1, :]
                tie = jnp.where(gm_m == gm, jnp.where(nidx > m, 1.0, 0.0), 0.0)
                rank = rank + jnp.where(gm_m > gm, 1.0, tie)
            sel = jnp.where(rank < min(MOBA_TOPK, nb), jnp.where(past, 1.0, 0.0), 0.0)
            for qb in range(nb):
                sel_ref[hh, qb] = sel[:, qb * blk:(qb + 1) * blk]

    def scores(qi, slot):
        past = qi * blk
        for hh in range(hp):
            cs = slice(hh * dh, (hh + 1) * dh)
            q = q_ref[past:past + blk, cs]
            s_diag = lax.dot_general(k_ref[past:past + blk, cs], q, NT_DIMS,
                                     preferred_element_type=F32) + bias_ref[hh, 0]
            s_diag = jnp.where(causal, s_diag, NEG_INF)
            s_ref[slot, hh, past:past + blk, :] = s_diag
            m = jnp.max(s_diag, axis=0, keepdims=True)
            if past:
                s_past = lax.dot_general(k_ref[0:past, cs], q, NT_DIMS,
                                         preferred_element_type=F32)
                for j in range(qi):
                    s_j = s_past[j * blk:(j + 1) * blk]
                    if j == qi - 1:
                        s_j = s_j + bias_ref[hh, 1]
                    sel = sel_ref[hh, qi, j:j + 1, :]
                    s_j = jnp.where(sel > 0.5, s_j, NEG_INF)
                    s_ref[slot, hh, j * blk:(j + 1) * blk, :] = s_j
                    m = jnp.maximum(m, jnp.max(s_j, axis=0, keepdims=True))
            m_ref[slot, hh] = m

    def attend(qi, slot):
        kv_len = (qi + 1) * blk
        for hh in range(hp):
            cs = slice(hh * dh, (hh + 1) * dh)
            p = jnp.exp2((s_ref[slot, hh, 0:kv_len, :] - m_ref[slot, hh]) * (scale * LOG2E))
            l = jnp.sum(p, axis=0, keepdims=True)
            p_ref[hh, 0:kv_len, :] = p.astype(BF16)
            acc = lax.dot_general(v_ref[0:kv_len, cs], p_ref[hh, 0:kv_len, :], TN_DIMS,
                                  preferred_element_type=F32)
            o_ref[:, cs] = (acc / l).T.astype(BF16)

    for r in range(nb):
        @pl.when(n % nb == r)
        def _(r=r):
            if r == 0:
                select_blocks()
            scores(r, r % 2)
            attend((r - 1) % nb, (r - 1) % 2)


def _moba_attn(qkv, bias, b, s, hp=4):
    assert MOBA_BLOCK + 1 >= REL_MAX_DIST and s % MOBA_BLOCK == 0
    t = b * s
    blk = MOBA_BLOCK
    nb = s // blk
    assert nb % 2 == 0
    dh = MOBA_HEAD_DIM
    groups = MOBA_HEADS // hp
    units = b * groups * nb

    def unit(n):
        return n // (groups * nb), (n // nb) % groups, n % nb

    def cur(n):
        return unit(jnp.minimum(n, units - 1))

    def prev(n):
        return unit(jnp.maximum(n - 1, 0))

    def q_map(n):
        bb, g, _ = cur(n)
        return bb, g

    def k_map(n):
        bb, g, _ = cur(n)
        return bb, groups + g

    def bias_map(n):
        return cur(n)[1], 0, 0, 0

    def v_map(n):
        bb, g, _ = prev(n)
        return bb, 2 * groups + g

    def o_map(n):
        bb, g, i = prev(n)
        return bb * nb + i, g

    return pl.pallas_call(
        functools.partial(_moba_attn_kernel, hp=hp, nb=nb, scale=dh ** -0.5),
        grid=(units + 1,),
        in_specs=[pl.BlockSpec((s, hp * dh), q_map),
                  pl.BlockSpec((s, hp * dh), k_map),
                  pl.BlockSpec((s, hp * dh), v_map),
                  pl.BlockSpec((hp, 2, blk, blk), bias_map)],
        out_specs=pl.BlockSpec((blk, hp * dh), o_map),
        out_shape=jax.ShapeDtypeStruct((t, MOBA_HEADS * dh), BF16),
        scratch_shapes=[pltpu.VMEM((hp, nb, nb, blk), F32),
                        pltpu.VMEM((2, hp, s, blk), F32),
                        pltpu.VMEM((2, hp, 1, blk), F32),
                        pltpu.VMEM((hp, s, blk), BF16)],
        compiler_params=_params(("arbitrary",)),
        name="moba_attn",
    )(qkv, qkv, qkv, bias)


SSD_GN = SSD_GROUPS * SSD_STATE
SSD_GW = SSD_INNER // SSD_GROUPS


def _softplus(x):
    return jnp.maximum(x, 0.0) + jnp.log1p(jnp.exp(-jnp.abs(x)))


def _silu(x):
    return x * jax.nn.sigmoid(x)


def _expand_heads(v, e3):
    h1 = v.astype(BF16)
    r1 = v - h1.astype(F32)
    h2 = r1.astype(BF16)
    h3 = (r1 - h2.astype(F32)).astype(BF16)
    return jnp.dot(jnp.concatenate([h1, h2, h3], axis=1), e3, preferred_element_type=F32)


def _ssd_kernel(z_ref, xbc_ref, dt_ref, cw_ref, cb_ref, dtb_ref, alog_ref, dexp_ref,
                nw_ref, e3_ref, o_ref, xpad_ref, st_ref):
    c = pl.program_id(1)
    L = SSD_CHUNK
    P = SSD_HEAD_DIM
    N = SSD_STATE
    halo = SUBLANES

    @pl.when(c == 0)
    def _():
        xpad_ref[0:halo, :] = jnp.zeros((halo, SSD_CONV_DIM), F32)
        st_ref[...] = jnp.zeros_like(st_ref)

    xpad_ref[halo:halo + L, :] = xbc_ref[...]
    conv = cb_ref[...] + cw_ref[SSD_CONV - 1:SSD_CONV, :] * xbc_ref[...]
    for k in range(SSD_CONV - 1):
        r0 = halo - (SSD_CONV - 1) + k
        conv = conv + cw_ref[k:k + 1, :] * xpad_ref[r0:r0 + L, :]
    xpad_ref[0:halo, :] = xpad_ref[L:L + halo, :]
    xc = _silu(conv)
    xs = xc[:, :SSD_INNER]
    bm = xc[:, SSD_INNER:SSD_INNER + SSD_GN].astype(BF16)
    cm = xc[:, SSD_INNER + SSD_GN:].astype(BF16)
    xs_b = xs.astype(BF16)

    dt = _softplus(dt_ref[...] + dtb_ref[...])
    a = -jnp.exp(alog_ref[...])
    row = lax.broadcasted_iota(jnp.int32, (L, L), 0)
    col = lax.broadcasted_iota(jnp.int32, (L, L), 1)
    tril = row >= col
    a_cs = jnp.dot(jnp.where(tril, 1.0, 0.0).astype(F32), dt * a,
                   precision=HIGHEST, preferred_element_type=F32)
    a_cs_t = a_cs.T
    dt_t = dt.T
    e3 = e3_ref[...]
    decay_exp = _expand_heads(jnp.exp(a_cs), e3)
    w_exp = _expand_heads(jnp.exp(a_cs[L - 1:L, :] - a_cs) * dt, e3)
    xw = (xs * w_exp).astype(BF16)
    lane = lax.broadcasted_iota(jnp.int32, (L, LANES), 1)

    ys = []
    for g in range(SSD_GROUPS):
        gs = slice(g * SSD_GW, (g + 1) * SSD_GW)
        b_g = bm[:, g * N:(g + 1) * N]
        c_g = cm[:, g * N:(g + 1) * N]
        cb = lax.dot_general(c_g, b_g, NT_DIMS, preferred_element_type=F32)
        state = st_ref[:, gs]
        y_off = jnp.dot(c_g, state.astype(BF16), preferred_element_type=F32) * decay_exp[:, gs]
        new_state = lax.dot_general(b_g, xw[:, gs], TN_DIMS, preferred_element_type=F32)
        st_ref[:, gs] = state * decay_exp[L - 1:L, gs] + new_state
        y_diag = []
        for pr in range(SSD_HEADS_PER_GROUP // 2):
            h0 = g * SSD_HEADS_PER_GROUP + 2 * pr
            mats = []
            for hr in (h0, h0 + 1):
                seg = a_cs[:, hr:hr + 1] - a_cs_t[hr:hr + 1, :]
                decay = jnp.exp(jnp.where(tril, seg, NEG_INF))
                mats.append((cb * decay * dt_t[hr:hr + 1, :]).astype(BF16))
            xp = xs_b[:, h0 * P:(h0 + 2) * P]
            zero = jnp.zeros_like(xp)
            x2 = jnp.concatenate([jnp.where(lane < P, xp, zero),
                                  jnp.where(lane >= P, xp, zero)], axis=0)
            y_diag.append(jnp.dot(jnp.concatenate(mats, axis=1), x2,
                                  preferred_element_type=F32))
        ys.append(jnp.concatenate(y_diag, axis=1) + y_off)
    y = jnp.concatenate(ys, axis=1) + dexp_ref[...] * xs
    gated = y * _silu(z_ref[...])
    outs = []
    for g in range(SSD_GROUPS):
        gg = gated[:, g * SSD_GW:(g + 1) * SSD_GW]
        ms = jnp.mean(gg * gg, axis=-1, keepdims=True)
        outs.append(gg * lax.rsqrt(ms + NORM_EPS))
    o_ref[...] = (jnp.concatenate(outs, axis=1) * nw_ref[...]).astype(BF16)


def _ssd_scan(z, xbc, dt, conv_w, conv_b, dt_bias, a_log, d_skip, norm_w, b, s):
    t = b * s
    L = SSD_CHUNK
    nc = s // L
    pad_h = lambda v: jnp.pad(v, (0, LANES - SSD_HEADS)).reshape(1, LANES)
    e = np.zeros((LANES, SSD_INNER), np.float32)
    for r in range(SSD_HEADS):
        e[r, r * SSD_HEAD_DIM:(r + 1) * SSD_HEAD_DIM] = 1.0
    e3 = jnp.asarray(np.concatenate([e, e, e], axis=0), dtype=BF16)
    d_exp = jnp.repeat(d_skip, SSD_HEAD_DIM).reshape(1, SSD_INNER)
    row = lambda bb, c: (bb * nc + c, 0)
    fixed = lambda bb, c: (0, 0)
    return pl.pallas_call(
        _ssd_kernel,
        grid=(b, nc),
        in_specs=[pl.BlockSpec((L, SSD_INNER), row),
                  pl.BlockSpec((L, SSD_CONV_DIM), row),
                  pl.BlockSpec((L, LANES), row),
                  pl.BlockSpec((SSD_CONV, SSD_CONV_DIM), fixed),
                  pl.BlockSpec((1, SSD_CONV_DIM), fixed),
                  pl.BlockSpec((1, LANES), fixed),
                  pl.BlockSpec((1, LANES), fixed),
                  pl.BlockSpec((1, SSD_INNER), fixed),
                  pl.BlockSpec((1, SSD_INNER), fixed),
                  pl.BlockSpec((3 * LANES, SSD_INNER), fixed)],
        out_specs=pl.BlockSpec((L, SSD_INNER), row),
        out_shape=jax.ShapeDtypeStruct((t, SSD_INNER), BF16),
        scratch_shapes=[pltpu.VMEM((L + SUBLANES, SSD_CONV_DIM), F32),
                        pltpu.VMEM((SSD_STATE, SSD_INNER), F32)],
        compiler_params=_params(("parallel", "arbitrary")),
        name="ssd_scan",
    )(z, xbc, dt, conv_w, conv_b.reshape(1, -1), pad_h(dt_bias), pad_h(a_log), d_exp,
      norm_w.reshape(1, -1), e3)


def _post_kernel(x_ref, a_ref, wo_ref, g_ref, w1_ref, w2_ref, gf_ref, o_ref,
                 x1_ref, h_ref, acc_ref, *, final_norm):
    j = pl.program_id(1)

    @pl.when(j == 0)
    def _():
        x1 = x_ref[...] + jnp.dot(a_ref[...], wo_ref[...], preferred_element_type=F32)
        x1_ref[...] = x1
        h_ref[...] = _rms(x1, g_ref[...]).astype(BF16)
        acc_ref[...] = jnp.zeros_like(acc_ref)

    u = jnp.maximum(jnp.dot(h_ref[...], w1_ref[...], preferred_element_type=F32), 0.0)
    acc_ref[...] += jnp.dot((u * u).astype(BF16), w2_ref[...], preferred_element_type=F32)

    @pl.when(j == pl.num_programs(1) - 1)
    def _():
        y = x1_ref[...] + acc_ref[...]
        if final_norm:
            y = _rms(y, gf_ref[...])
        o_ref[...] = y


def _post(x, a, w_o, g, w1, w2, g_final, final_norm, tm=512, th=1024):
    t, d = x.shape
    ka = a.shape[1]
    hid = w1.shape[1]
    return pl.pallas_call(
        functools.partial(_post_kernel, final_norm=final_norm),
        grid=(t // tm, hid // th),
        in_specs=[pl.BlockSpec((tm, d), lambda i, j: (i, 0)),
                  pl.BlockSpec((tm, ka), lambda i, j: (i, 0)),
                  pl.BlockSpec((ka, d), lambda i, j: (0, 0)),
                  pl.BlockSpec((1, d), lambda i, j: (0, 0)),
                  pl.BlockSpec((d, th), lambda i, j: (0, j)),
                  pl.BlockSpec((th, d), lambda i, j: (j, 0)),
                  pl.BlockSpec((1, d), lambda i, j: (0, 0))],
        out_specs=pl.BlockSpec((tm, d), lambda i, j: (i, 0)),
        out_shape=jax.ShapeDtypeStruct((t, d), F32),
        scratch_shapes=[pltpu.VMEM((tm, d), F32), pltpu.VMEM((tm, d), BF16),
                        pltpu.VMEM((tm, d), F32)],
        compiler_params=_params(("parallel", "arbitrary"), 56),
        name="post_mlp",
    )(x, a, w_o.astype(BF16), g.reshape(1, d), w1.astype(BF16), w2.astype(BF16),
      g_final.reshape(1, d))


def _mla_mixer(x, g, rope, w_in, q_norm, w_uq, kv_norm, w_ukv, b, s):
    q, kv, kpe = _mla_proj(x, g, w_in, q_norm, w_uq, kv_norm, w_ukv, rope)
    return _mla_attn(q, kv, kpe, b, s)


def _moba_mixer(x, g, w_qkv, bias, b, s):
    n = w_qkv.shape[1]
    (qkv,) = _norm_proj(x, g, w_qkv.astype(BF16), (n,), (BF16,), tm=512)
    return _moba_attn(qkv, bias, b, s)


def _ssd_mixer(x, g, w_in, conv_w, conv_b, dt_bias, a_log, d_skip, norm_w, b, s):
    w_pad = jnp.pad(w_in, ((0, 0), (0, LANES - SSD_HEADS))).astype(BF16)
    z, xbc, dt = _norm_proj(x, g, w_pad, (SSD_INNER, SSD_CONV_DIM, LANES),
                            (F32, F32, F32), tm=256)
    return _ssd_scan(z, xbc, dt, conv_w, conv_b, dt_bias, a_log, d_skip, norm_w, b, s)


def kernel(x, positions, norm_mix, norm_mlp, norm_final, rel_bias, mla_w_in, mla_q_norm, mla_w_uq, mla_kv_norm, mla_w_ukv, mla_w_o, moba_w_qkv, moba_w_o, ssd_w_in, ssd_conv_w, ssd_conv_b, ssd_dt_bias, ssd_a_log, ssd_d, ssd_norm, ssd_w_out, mlp_w1, mlp_w2):
    b, s, d = x.shape
    xf = x.reshape(b * s, d)
    rope = None
    bias = None
    for i in range(DEPTH):
        kind, j = i % N_MIXERS, i // N_MIXERS
        if kind == 0:
            if rope is None:
                rope = _rope_tables(positions)
            a = _mla_mixer(xf, norm_mix[i], rope, mla_w_in[j], mla_q_norm[j], mla_w_uq[j],
                           mla_kv_norm[j], mla_w_ukv[j], b, s)
            w_o = mla_w_o[j]
        elif kind == 1:
            if bias is None:
                bias = _moba_bias(rel_bias)
            a = _moba_mixer(xf, norm_mix[i], moba_w_qkv[j], bias, b, s)
            w_o = moba_w_o[j]
        else:
            a = _ssd_mixer(xf, norm_mix[i], ssd_w_in[j], ssd_conv_w[j], ssd_conv_b[j],
                           ssd_dt_bias[j], ssd_a_log[j], ssd_d[j], ssd_norm[j], b, s)
            w_o = ssd_w_out[j]
        xf = _post(xf, a, w_o, norm_mlp[i], mlp_w1[i], mlp_w2[i], norm_final,
                   final_norm=(i == DEPTH - 1))
    return xf.reshape(b, s, d)
```

```python
import functools
import math

import numpy as np
import jax
import jax.numpy as jnp
from jax import lax
from jax.experimental import pallas as pl
from jax.experimental.pallas import tpu as pltpu

F32 = jnp.float32
BF16 = jnp.bfloat16
HIGHEST = lax.Precision.HIGHEST

D_MODEL = 1024
DEPTH = 4
N_MIXERS = 3
NORM_EPS = 1e-6
NEG_INF = -1e30

MLA_HEADS = 16
MLA_Q_RANK = 256
MLA_KV_RANK = 256
MLA_NOPE = 128
MLA_ROPE = 64
MLA_V = 128
ROPE_BASE = 10000.0

MOBA_HEADS = 8
MOBA_HEAD_DIM = D_MODEL // MOBA_HEADS
MOBA_BLOCK = 256
MOBA_TOPK = 3
REL_BUCKETS = 32
REL_MAX_DIST = 128

SSD_INNER = 2 * D_MODEL
SSD_HEAD_DIM = 64
SSD_HEADS = SSD_INNER // SSD_HEAD_DIM
SSD_GROUPS = 2
SSD_HEADS_PER_GROUP = SSD_HEADS // SSD_GROUPS
SSD_STATE = 128
SSD_CONV = 4
SSD_CONV_DIM = SSD_INNER + 2 * SSD_GROUPS * SSD_STATE
SSD_CHUNK = 128

MLP_HIDDEN = 4 * D_MODEL

LANES = 128
SUBLANES = 8
MIB = 1024 * 1024

NT_DIMS = (((1,), (1,)), ((), ()))
TN_DIMS = (((0,), (0,)), ((), ()))


def _params(semantics, vmem_mib=48):
    return pltpu.CompilerParams(dimension_semantics=semantics,
                                vmem_limit_bytes=vmem_mib * MIB)


def _rms(x, g):
    ms = jnp.mean(x * x, axis=-1, keepdims=True)
    return x * lax.rsqrt(ms + NORM_EPS) * g


def _norm_proj_kernel(x_ref, g_ref, w_ref, *o_refs, splits, col_chunk):
    h = _rms(x_ref[...], g_ref[...]).astype(BF16)
    off = 0
    for o_ref, n in zip(o_refs, splits):
        for c0 in range(0, n, col_chunk):
            c1 = min(c0 + col_chunk, n)
            o_ref[:, c0:c1] = jnp.dot(
                h, w_ref[:, off + c0:off + c1],
                preferred_element_type=F32).astype(o_ref.dtype)
        off += n


def _norm_proj(x, g, w, splits, dtypes, tm):
    t, d = x.shape
    n = w.shape[1]
    assert sum(splits) == n and t % tm == 0
    return pl.pallas_call(
        functools.partial(_norm_proj_kernel, splits=tuple(splits), col_chunk=1024),
        grid=(t // tm,),
        in_specs=[pl.BlockSpec((tm, d), lambda i: (i, 0)),
                  pl.BlockSpec((1, d), lambda i: (0, 0)),
                  pl.BlockSpec((d, n), lambda i: (0, 0))],
        out_specs=[pl.BlockSpec((tm, s), lambda i: (i, 0)) for s in splits],
        out_shape=[jax.ShapeDtypeStruct((t, s), dt) for s, dt in zip(splits, dtypes)],
        compiler_params=_params(("parallel",), 56),
        name="norm_proj",
    )(x, g.reshape(1, d), w)


def _rope_table_kernel(pos_ref, freq_ref, cos_ref, sin_ref):
    ang = pos_ref[...].astype(F32) * freq_ref[...]
    cos_ref[...] = jnp.cos(ang)
    sin_ref[...] = jnp.sin(ang)


def _rope_tables(positions):
    b, s = positions.shape
    t = b * s
    half = MLA_ROPE // 2
    per_row = LANES // half
    rows = t // per_row
    inv_freq = ROPE_BASE ** (-(jnp.arange(0, MLA_ROPE, 2, dtype=F32) / MLA_ROPE))
    pos_rep = jnp.repeat(positions.reshape(rows, per_row), half, axis=1)
    freq = jnp.tile(inv_freq, per_row).reshape(1, LANES)
    tr = 512
    cos, sin = pl.pallas_call(
        _rope_table_kernel,
        grid=(rows // tr,),
        in_specs=[pl.BlockSpec((tr, LANES), lambda i: (i, 0)),
                  pl.BlockSpec((1, LANES), lambda i: (0, 0))],
        out_specs=[pl.BlockSpec((tr, LANES), lambda i: (i, 0))] * 2,
        out_shape=[jax.ShapeDtypeStruct((rows, LANES), F32)] * 2,
        compiler_params=_params(("parallel",)),
        name="rope_tables",
    )(pos_rep, freq)
    cos = cos.reshape(t, half)
    sin = sin.reshape(t, half)
    zeros = lambda n: jnp.zeros((t, n), F32)
    cos_t = jnp.concatenate([cos, cos, zeros(LANES - 2 * half)], axis=1)
    sin_a = jnp.concatenate([-sin, zeros(LANES - half)], axis=1)
    sin_b = jnp.concatenate([zeros(half), sin, zeros(LANES - 2 * half)], axis=1)
    return cos_t, sin_a, sin_b


def _rope128(v, cos_t, sin_a, sin_b):
    half = MLA_ROPE // 2
    return (v * cos_t + pltpu.roll(v, LANES - half, 1) * sin_a
            + pltpu.roll(v, half, 1) * sin_b)


MLA_QW = 2 * LANES
MLA_KVW = MLA_NOPE + MLA_V
MLA_IN_PAD = MLA_Q_RANK + MLA_KV_RANK + LANES


def _mla_proj_kernel(x_ref, g_ref, win_ref, qn_ref, kvn_ref, wuq_ref, wukv_ref,
                     cos_ref, sa_ref, sb_ref, q_ref, kv_ref, kpe_ref):
    h = _rms(x_ref[...], g_ref[...]).astype(BF16)
    proj = jnp.dot(h, win_ref[...], preferred_element_type=F32)
    cq = _rms(proj[:, :MLA_Q_RANK], qn_ref[...]).astype(BF16)
    ckv = _rms(proj[:, MLA_Q_RANK:MLA_Q_RANK + MLA_KV_RANK], kvn_ref[...]).astype(BF16)
    cos_t, sin_a, sin_b = cos_ref[...], sa_ref[...], sb_ref[...]
    kpe_ref[...] = _rope128(proj[:, MLA_Q_RANK + MLA_KV_RANK:], cos_t, sin_a, sin_b).astype(BF16)
    for hh in range(MLA_HEADS):
        c0 = hh * MLA_QW
        qh = jnp.dot(cq, wuq_ref[:, c0:c0 + MLA_QW], preferred_element_type=F32)
        q_ref[:, c0:c0 + MLA_NOPE] = qh[:, :MLA_NOPE].astype(BF16)
        q_ref[:, c0 + MLA_NOPE:c0 + MLA_QW] = _rope128(
            qh[:, MLA_NOPE:], cos_t, sin_a, sin_b).astype(BF16)
    n_kv = MLA_HEADS * MLA_KVW
    for c0 in range(0, n_kv, 1024):
        kv_ref[:, c0:c0 + 1024] = jnp.dot(
            ckv, wukv_ref[:, c0:c0 + 1024], preferred_element_type=F32).astype(BF16)


def _mla_proj(x, g, w_in, q_norm, w_uq, kv_norm, w_ukv, rope, tm=256):
    t, d = x.shape
    nq = MLA_HEADS * MLA_QW
    nkv = MLA_HEADS * MLA_KVW
    w_in_p = jnp.pad(w_in, ((0, 0), (0, MLA_IN_PAD - w_in.shape[1]))).astype(BF16)
    w_uq_p = jnp.pad(w_uq.reshape(MLA_Q_RANK, MLA_HEADS, MLA_NOPE + MLA_ROPE),
                     ((0, 0), (0, 0), (0, MLA_QW - MLA_NOPE - MLA_ROPE))
                     ).reshape(MLA_Q_RANK, nq).astype(BF16)
    w_ukv_p = jnp.swapaxes(w_ukv.reshape(MLA_KV_RANK, MLA_HEADS, 2, MLA_NOPE), 1, 2
                           ).reshape(MLA_KV_RANK, nkv).astype(BF16)
    row = lambda i: (i, 0)
    fixed = lambda i: (0, 0)
    return pl.pallas_call(
        _mla_proj_kernel,
        grid=(t // tm,),
        in_specs=[pl.BlockSpec((tm, d), row),
                  pl.BlockSpec((1, d), fixed),
                  pl.BlockSpec((d, MLA_IN_PAD), fixed),
                  pl.BlockSpec((1, MLA_Q_RANK), fixed),
                  pl.BlockSpec((1, MLA_KV_RANK), fixed),
                  pl.BlockSpec((MLA_Q_RANK, nq), fixed),
                  pl.BlockSpec((MLA_KV_RANK, nkv), fixed),
                  pl.BlockSpec((tm, LANES), row),
                  pl.BlockSpec((tm, LANES), row),
                  pl.BlockSpec((tm, LANES), row)],
        out_specs=[pl.BlockSpec((tm, nq), row),
                   pl.BlockSpec((tm, nkv), row),
                   pl.BlockSpec((tm, LANES), row)],
        out_shape=[jax.ShapeDtypeStruct((t, nq), BF16),
                   jax.ShapeDtypeStruct((t, nkv), BF16),
                   jax.ShapeDtypeStruct((t, LANES), BF16)],
        compiler_params=_params(("parallel",)),
        name="mla_proj",
    )(x, g.reshape(1, d), w_in_p, q_norm.reshape(1, -1), kv_norm.reshape(1, -1),
      w_uq_p, w_ukv_p, *rope)


LOG2E = math.log2(math.e)


def _mla_attn_kernel(q_ref, k_ref, kpe_ref, v_ref, o_ref, s_ref, m_ref,
                     *, hp, tq, nq, scale):
    n = pl.program_id(0)
    c = scale * LOG2E
    kidx = lax.broadcasted_iota(jnp.int32, (tq, tq), 0)
    qidx = lax.broadcasted_iota(jnp.int32, (tq, tq), 1)
    causal = kidx <= qidx

    @pl.when(n == 0)
    def _():
        s_ref[...] = jnp.zeros_like(s_ref)
        m_ref[...] = jnp.zeros_like(m_ref)

    def scores(qi, slot):
        past = qi * tq
        kpe = kpe_ref[0:past + tq, :]
        for hh in range(hp):
            kc = jnp.concatenate(
                [k_ref[0:past + tq, hh * MLA_NOPE:(hh + 1) * MLA_NOPE], kpe], axis=1)
            q = q_ref[:, hh * MLA_QW:(hh + 1) * MLA_QW]
            s_diag = jnp.where(
                causal, lax.dot_general(kc[past:], q, NT_DIMS, preferred_element_type=F32),
                NEG_INF)
            s_ref[slot, hh, past:past + tq, :] = s_diag
            m = jnp.max(s_diag, axis=0, keepdims=True)
            if past:
                s_past = lax.dot_general(kc[:past], q, NT_DIMS, preferred_element_type=F32)
                s_ref[slot, hh, 0:past, :] = s_past
                m = jnp.maximum(m, jnp.max(s_past, axis=0, keepdims=True))
            m_ref[slot, hh] = m

    def attend(qi, slot):
        kv_len = (qi + 1) * tq
        for hh in range(hp):
            p = jnp.exp2((s_ref[slot, hh, 0:kv_len, :] - m_ref[slot, hh]) * c)
            l = jnp.sum(p, axis=0, keepdims=True)
            acc = lax.dot_general(v_ref[0:kv_len, hh * MLA_V:(hh + 1) * MLA_V],
                                  p.astype(BF16), TN_DIMS, preferred_element_type=F32)
            o_ref[:, hh * MLA_V:(hh + 1) * MLA_V] = (acc / l).T.astype(BF16)

    for r in range(nq):
        @pl.when(n % nq == r)
        def _(r=r):
            scores(r, r % 2)
            attend((r - 1) % nq, (r - 1) % 2)


def _mla_attn(q, kv, kpe, b, s, hp=4, tq=256):
    assert MLA_NOPE == MLA_V
    t = b * s
    nq = s // tq
    assert nq % 2 == 0
    groups = MLA_HEADS // hp
    units = b * groups * nq
    scale = (MLA_NOPE + MLA_ROPE) ** -0.5

    def unit(n):
        return n // (groups * nq), (n // nq) % groups, n % nq

    def cur(n):
        return unit(jnp.minimum(n, units - 1))

    def last(n):
        return unit(jnp.maximum(n - 1, 0))

    def q_map(n):
        bb, g, i = cur(n)
        return bb * nq + i, g

    def k_map(n):
        bb, g, _ = cur(n)
        return bb, g

    def kpe_map(n):
        return cur(n)[0], 0

    def v_map(n):
        bb, g, _ = last(n)
        return bb, groups + g

    def o_map(n):
        bb, g, i = last(n)
        return bb * nq + i, g

    return pl.pallas_call(
        functools.partial(_mla_attn_kernel, hp=hp, tq=tq, nq=nq, scale=scale),
        grid=(units + 1,),
        in_specs=[pl.BlockSpec((tq, hp * MLA_QW), q_map),
                  pl.BlockSpec((s, hp * MLA_NOPE), k_map),
                  pl.BlockSpec((s, LANES), kpe_map),
                  pl.BlockSpec((s, hp * MLA_V), v_map)],
        out_specs=pl.BlockSpec((tq, hp * MLA_V), o_map),
        out_shape=jax.ShapeDtypeStruct((t, MLA_HEADS * MLA_V), BF16),
        scratch_shapes=[pltpu.VMEM((2, hp, s, tq), F32),
                        pltpu.VMEM((2, hp, 1, tq), F32)],
        compiler_params=_params(("arbitrary",)),
        name="mla_attn",
    )(q, kv, kpe, kv)


def _t5_bucket_np(n):
    max_exact = REL_BUCKETS // 2
    nf = np.maximum(n, max_exact).astype(np.float32)
    large = max_exact + (np.log(nf / np.float32(max_exact))
                         / np.float32(math.log(REL_MAX_DIST / max_exact))
                         * np.float32(REL_BUCKETS - max_exact)).astype(np.int32)
    large = np.minimum(large, REL_BUCKETS - 1)
    return np.where(n < max_exact, n, large).astype(np.int32)


def _moba_bias_kernel(tab_ref, bucket_ref, o_ref, *, inv_scale):
    h = pl.program_id(0)
    bucket = bucket_ref[...]
    far = tab_ref[REL_BUCKETS - 1, h]
    acc = jnp.zeros(bucket.shape, F32)
    for k in range(REL_BUCKETS - 1):
        acc = jnp.where(bucket == k, tab_ref[k, h] - far, acc)
    o_ref[0] = acc * inv_scale


def _moba_bias(rel_bias):
    blk = MOBA_BLOCK
    key = np.arange(blk)[:, None]
    qry = np.arange(blk)[None, :]
    dist = np.stack([np.maximum(qry - key, 0), blk + qry - key])
    bucket = jnp.asarray(_t5_bucket_np(dist))
    return pl.pallas_call(
        functools.partial(_moba_bias_kernel, inv_scale=MOBA_HEAD_DIM ** 0.5),
        grid=(MOBA_HEADS,),
        in_specs=[pl.BlockSpec(memory_space=pltpu.SMEM),
                  pl.BlockSpec((2, blk, blk), lambda h: (0, 0, 0))],
        out_specs=pl.BlockSpec((1, 2, blk, blk), lambda h: (h, 0, 0, 0)),
        out_shape=jax.ShapeDtypeStruct((MOBA_HEADS, 2, blk, blk), F32),
        compiler_params=_params(("arbitrary",)),
        name="moba_bias",
    )(rel_bias, bucket)


def _moba_attn_kernel(q_ref, k_ref, v_ref, bias_ref, o_ref,
                      sel_ref, s_ref, m_ref, p_ref, *, hp, nb, scale):
    n = pl.program_id(0)
    blk = MOBA_BLOCK
    dh = MOBA_HEAD_DIM
    s_len = nb * blk
    kidx = lax.broadcasted_iota(jnp.int32, (blk, blk), 0)
    qidx = lax.broadcasted_iota(jnp.int32, (blk, blk), 1)
    causal = kidx <= qidx

    @pl.when(n == 0)
    def _():
        s_ref[...] = jnp.zeros_like(s_ref)
        m_ref[...] = jnp.zeros_like(m_ref)

    def select_blocks():
        nidx = lax.broadcasted_iota(jnp.int32, (nb, s_len), 0)
        qblk = lax.broadcasted_iota(jnp.int32, (nb, s_len), 1) // blk
        past = nidx < qblk
        for hh in range(hp):
            cs = slice(hh * dh, (hh + 1) * dh)
            k_mean = jnp.concatenate(
                [jnp.sum(k_ref[n * blk:(n + 1) * blk, cs].astype(F32), axis=0, keepdims=True)
                 for n in range(nb)], axis=0) / blk
            h1 = k_mean.astype(BF16)
            r1 = k_mean - h1.astype(F32)
            h2 = r1.astype(BF16)
            h3 = (r1 - h2.astype(F32)).astype(BF16)
            g3 = lax.dot_general(jnp.concatenate([h1, h2, h3], axis=0), q_ref[:, cs], NT_DIMS,
                                 preferred_element_type=F32)
            gate = g3[0:nb] + g3[nb:2 * nb] + g3[2 * nb:3 * nb]
            gm = jnp.where(past, gate, NEG_INF)
            rank = jnp.zeros((nb, s_len), F32)
            for m in range(nb):
                gm_m = gm[m:m + 1, :]
                tie = jnp.where(gm_m == gm, jnp.where(nidx > m, 1.0, 0.0), 0.0)
                rank = rank + jnp.where(gm_m > gm, 1.0, tie)
            sel = jnp.where(rank < min(MOBA_TOPK, nb), jnp.where(past, 1.0, 0.0), 0.0)
            for qb in range(nb):
                sel_ref[hh, qb] = sel[:, qb * blk:(qb + 1) * blk]

    def scores(qi, slot):
        past = qi * blk
        for hh in range(hp):
            cs = slice(hh * dh, (hh + 1) * dh)
            q = q_ref[past:past + blk, cs]
            s_diag = lax.dot_general(k_ref[past:past + blk, cs], q, NT_DIMS,
                                     preferred_element_type=F32) + bias_ref[hh, 0]
            s_diag = jnp.where(causal, s_diag, NEG_INF)
            s_ref[slot, hh, past:past + blk, :] = s_diag
            m = jnp.max(s_diag, axis=0, keepdims=True)
            if past:
                s_past = lax.dot_general(k_ref[0:past, cs], q, NT_DIMS,
                                         preferred_element_type=F32)
                for j in range(qi):
                    s_j = s_past[j * blk:(j + 1) * blk]
                    if j == qi - 1:
                        s_j = s_j + bias_ref[hh, 1]
                    sel = sel_ref[hh, qi, j:j + 1, :]
                    s_j = jnp.where(sel > 0.5, s_j, NEG_INF)
                    s_ref[slot, hh, j * blk:(j + 1) * blk, :] = s_j
                    m = jnp.maximum(m, jnp.max(s_j, axis=0, keepdims=True))
            m_ref[slot, hh] = m

    def attend(qi, slot):
        kv_len = (qi + 1) * blk
        for hh in range(hp):
            cs = slice(hh * dh, (hh + 1) * dh)
            p = jnp.exp2((s_ref[slot, hh, 0:kv_len, :] - m_ref[slot, hh]) * (scale * LOG2E))
            l = jnp.sum(p, axis=0, keepdims=True)
            p_ref[hh, 0:kv_len, :] = p.astype(BF16)
            acc = lax.dot_general(v_ref[0:kv_len, cs], p_ref[hh, 0:kv_len, :], TN_DIMS,
                                  preferred_element_type=F32)
            o_ref[:, cs] = (acc / l).T.astype(BF16)

    for r in range(nb):
        @pl.when(n % nb == r)
        def _(r=r):
            if r == 0:
                select_blocks()
            scores(r, r % 2)
            attend((r - 1) % nb, (r - 1) % 2)


def _moba_attn(qkv, bias, b, s, hp=4):
    assert MOBA_BLOCK + 1 >= REL_MAX_DIST and s % MOBA_BLOCK == 0
    t = b * s
    blk = MOBA_BLOCK
    nb = s // blk
    assert nb % 2 == 0
    dh = MOBA_HEAD_DIM
    groups = MOBA_HEADS // hp
    units = b * groups * nb

    def unit(n):
        return n // (groups * nb), (n // nb) % groups, n % nb

    def cur(n):
        return unit(jnp.minimum(n, units - 1))

    def prev(n):
        return unit(jnp.maximum(n - 1, 0))

    def q_map(n):
        bb, g, _ = cur(n)
        return bb, g

    def k_map(n):
        bb, g, _ = cur(n)
        return bb, groups + g

    def bias_map(n):
        return cur(n)[1], 0, 0, 0

    def v_map(n):
        bb, g, _ = prev(n)
        return bb, 2 * groups + g

    def o_map(n):
        bb, g, i = prev(n)
        return bb * nb + i, g

    return pl.pallas_call(
        functools.partial(_moba_attn_kernel, hp=hp, nb=nb, scale=dh ** -0.5),
        grid=(units + 1,),
        in_specs=[pl.BlockSpec((s, hp * dh), q_map),
                  pl.BlockSpec((s, hp * dh), k_map),
                  pl.BlockSpec((s, hp * dh), v_map),
                  pl.BlockSpec((hp, 2, blk, blk), bias_map)],
        out_specs=pl.BlockSpec((blk, hp * dh), o_map),
        out_shape=jax.ShapeDtypeStruct((t, MOBA_HEADS * dh), BF16),
        scratch_shapes=[pltpu.VMEM((hp, nb, nb, blk), F32),
                        pltpu.VMEM((2, hp, s, blk), F32),
                        pltpu.VMEM((2, hp, 1, blk), F32),
                        pltpu.VMEM((hp, s, blk), BF16)],
        compiler_params=_params(("arbitrary",)),
        name="moba_attn",
    )(qkv, qkv, qkv, bias)


SSD_GN = SSD_GROUPS * SSD_STATE
SSD_GW = SSD_INNER // SSD_GROUPS


def _softplus(x):
    return jnp.maximum(x, 0.0) + jnp.log1p(jnp.exp(-jnp.abs(x)))


def _silu(x):
    return x * jax.nn.sigmoid(x)


def _expand_heads(v, e3):
    h1 = v.astype(BF16)
    r1 = v - h1.astype(F32)
    h2 = r1.astype(BF16)
    h3 = (r1 - h2.astype(F32)).astype(BF16)
    return jnp.dot(jnp.concatenate([h1, h2, h3], axis=1), e3, preferred_element_type=F32)


def _ssd_proj_kernel(x_ref, g_ref, w_ref, cw_ref, cb_ref, z_ref, xs_ref, bc_ref, dt_ref,
                     xpad_ref, *, tiles_per_seq):
    i = pl.program_id(0)
    tm = x_ref.shape[0]
    halo = SUBLANES
    c_xbc = SSD_INNER
    c_dt = SSD_INNER + SSD_CONV_DIM

    @pl.when(i % tiles_per_seq == 0)
    def _():
        xpad_ref[0:halo, :] = jnp.zeros((halo, SSD_CONV_DIM), F32)

    h = _rms(x_ref[...], g_ref[...]).astype(BF16)
    dt_ref[...] = jnp.dot(h, w_ref[:, c_dt:c_dt + LANES], preferred_element_type=F32)
    for c0 in range(0, SSD_CONV_DIM, 512):
        cs = slice(c0, c0 + 512)
        if c0 < SSD_INNER:
            z_ref[:, cs] = jnp.dot(h, w_ref[:, cs], preferred_element_type=F32)
        raw = jnp.dot(h, w_ref[:, c_xbc + c0:c_xbc + c0 + 512], preferred_element_type=F32)
        xpad_ref[halo:halo + tm, cs] = raw
        conv = cb_ref[:, cs] + cw_ref[SSD_CONV - 1:SSD_CONV, cs] * raw
        for k in range(SSD_CONV - 1):
            r0 = halo - (SSD_CONV - 1) + k
            conv = conv + cw_ref[k:k + 1, cs] * xpad_ref[r0:r0 + tm, cs]
        xpad_ref[0:halo, cs] = xpad_ref[tm:tm + halo, cs]
        xc = _silu(conv)
        if c0 < SSD_INNER:
            xs_ref[:, cs] = xc
        else:
            bc_ref[:, c0 - SSD_INNER:c0 - SSD_INNER + 512] = xc.astype(BF16)


def _ssd_proj(x, g, w_in, conv_w, conv_b, s, tm=256):
    t, d = x.shape
    assert SSD_CONV_DIM % 512 == 0 and SSD_INNER % 1024 == 0 and s % tm == 0
    w_pad = jnp.pad(w_in, ((0, 0), (0, LANES - SSD_HEADS))).astype(BF16)
    n = w_pad.shape[1]
    row = lambda i: (i, 0)
    fixed = lambda i: (0, 0)
    return pl.pallas_call(
        functools.partial(_ssd_proj_kernel, tiles_per_seq=s // tm),
        grid=(t // tm,),
        in_specs=[pl.BlockSpec((tm, d), row),
                  pl.BlockSpec((1, d), fixed),
                  pl.BlockSpec((d, n), fixed),
                  pl.BlockSpec((SSD_CONV, SSD_CONV_DIM), fixed),
                  pl.BlockSpec((1, SSD_CONV_DIM), fixed)],
        out_specs=[pl.BlockSpec((tm, SSD_INNER), row),
                   pl.BlockSpec((tm, SSD_INNER), row),
                   pl.BlockSpec((tm, 2 * SSD_GN), row),
                   pl.BlockSpec((tm, LANES), row)],
        out_shape=[jax.ShapeDtypeStruct((t, SSD_INNER), F32),
                   jax.ShapeDtypeStruct((t, SSD_INNER), F32),
                   jax.ShapeDtypeStruct((t, 2 * SSD_GN), BF16),
                   jax.ShapeDtypeStruct((t, LANES), F32)],
        scratch_shapes=[pltpu.VMEM((tm + SUBLANES, SSD_CONV_DIM), F32)],
        compiler_params=_params(("arbitrary",), 56),
        name="ssd_proj",
    )(x, g.reshape(1, d), w_pad, conv_w, conv_b.reshape(1, -1))


def _ssd_kernel(z_ref, xs_ref, bc_ref, dt_ref, dtb_ref, alog_ref, dexp_ref,
                nw_ref, e3_ref, o_ref, st_ref):
    c = pl.program_id(1)
    L = SSD_CHUNK
    P = SSD_HEAD_DIM
    N = SSD_STATE

    @pl.when(c == 0)
    def _():
        st_ref[...] = jnp.zeros_like(st_ref)

    xs = xs_ref[...]
    bm = bc_ref[:, :SSD_GN]
    cm = bc_ref[:, SSD_GN:]
    xs_b = xs.astype(BF16)

    dt = _softplus(dt_ref[...] + dtb_ref[...])
    a = -jnp.exp(alog_ref[...])
    row = lax.broadcasted_iota(jnp.int32, (L, L), 0)
    col = lax.broadcasted_iota(jnp.int32, (L, L), 1)
    tril = row >= col
    a_cs = jnp.dot(jnp.where(tril, 1.0, 0.0).astype(F32), dt * a,
                   precision=HIGHEST, preferred_element_type=F32)
    a_cs_t = a_cs.T
    dt_t = dt.T
    e3 = e3_ref[...]
    decay_exp = _expand_heads(jnp.exp(a_cs), e3)
    w_exp = _expand_heads(jnp.exp(a_cs[L - 1:L, :] - a_cs) * dt, e3)
    xw = (xs * w_exp).astype(BF16)
    lane = lax.broadcasted_iota(jnp.int32, (L, LANES), 1)

    ys = []
    for g in range(SSD_GROUPS):
        gs = slice(g * SSD_GW, (g + 1) * SSD_GW)
        b_g = bm[:, g * N:(g + 1) * N]
        c_g = cm[:, g * N:(g + 1) * N]
        cb = lax.dot_general(c_g, b_g, NT_DIMS, preferred_element_type=F32)
        state = st_ref[:, gs]
        y_off = jnp.dot(c_g, state.astype(BF16), preferred_element_type=F32) * decay_exp[:, gs]
        new_state = lax.dot_general(b_g, xw[:, gs], TN_DIMS, preferred_element_type=F32)
        st_ref[:, gs] = state * decay_exp[L - 1:L, gs] + new_state
        y_diag = []
        for pr in range(SSD_HEADS_PER_GROUP // 2):
            h0 = g * SSD_HEADS_PER_GROUP + 2 * pr
            mats = []
            for hr in (h0, h0 + 1):
                seg = a_cs[:, hr:hr + 1] - a_cs_t[hr:hr + 1, :]
                decay = jnp.exp(jnp.where(tril, seg, NEG_INF))
                mats.append((cb * decay * dt_t[hr:hr + 1, :]).astype(BF16))
            xp = xs_b[:, h0 * P:(h0 + 2) * P]
            zero = jnp.zeros_like(xp)
            x2 = jnp.concatenate([jnp.where(lane < P, xp, zero),
                                  jnp.where(lane >= P, xp, zero)], axis=0)
            y_diag.append(jnp.dot(jnp.concatenate(mats, axis=1), x2,
                                  preferred_element_type=F32))
        ys.append(jnp.concatenate(y_diag, axis=1) + y_off)
    y = jnp.concatenate(ys, axis=1) + dexp_ref[...] * xs
    gated = y * _silu(z_ref[...])
    outs = []
    for g in range(SSD_GROUPS):
        gg = gated[:, g * SSD_GW:(g + 1) * SSD_GW]
        ms = jnp.mean(gg * gg, axis=-1, keepdims=True)
        outs.append(gg * lax.rsqrt(ms + NORM_EPS))
    o_ref[...] = (jnp.concatenate(outs, axis=1) * nw_ref[...]).astype(BF16)


def _ssd_scan(z, xs, bc, dt, dt_bias, a_log, d_skip, norm_w, b, s):
    t = b * s
    L = SSD_CHUNK
    nc = s // L
    pad_h = lambda v: jnp.pad(v, (0, LANES - SSD_HEADS)).reshape(1, LANES)
    e = np.zeros((LANES, SSD_INNER), np.float32)
    for r in range(SSD_HEADS):
        e[r, r * SSD_HEAD_DIM:(r + 1) * SSD_HEAD_DIM] = 1.0
    e3 = jnp.asarray(np.concatenate([e, e, e], axis=0), dtype=BF16)
    d_exp = jnp.repeat(d_skip, SSD_HEAD_DIM).reshape(1, SSD_INNER)
    row = lambda bb, c: (bb * nc + c, 0)
    fixed = lambda bb, c: (0, 0)
    return pl.pallas_call(
        _ssd_kernel,
        grid=(b, nc),
        in_specs=[pl.BlockSpec((L, SSD_INNER), row),
                  pl.BlockSpec((L, SSD_INNER), row),
                  pl.BlockSpec((L, 2 * SSD_GN), row),
                  pl.BlockSpec((L, LANES), row),
                  pl.BlockSpec((1, LANES), fixed),
                  pl.BlockSpec((1, LANES), fixed),
                  pl.BlockSpec((1, SSD_INNER), fixed),
                  pl.BlockSpec((1, SSD_INNER), fixed),
                  pl.BlockSpec((3 * LANES, SSD_INNER), fixed)],
        out_specs=pl.BlockSpec((L, SSD_INNER), row),
        out_shape=jax.ShapeDtypeStruct((t, SSD_INNER), BF16),
        scratch_shapes=[pltpu.VMEM((SSD_STATE, SSD_INNER), F32)],
        compiler_params=_params(("parallel", "arbitrary")),
        name="ssd_scan",
    )(z, xs, bc, dt, pad_h(dt_bias), pad_h(a_log), d_exp, norm_w.reshape(1, -1), e3)


def _post_kernel(x_ref, a_ref, wo_ref, g_ref, w1_ref, w2_ref, gf_ref, o_ref,
                 *, final_norm, th):
    x1 = x_ref[...] + jnp.dot(a_ref[...], wo_ref[...], preferred_element_type=F32)
    h = _rms(x1, g_ref[...]).astype(BF16)
    y = x1
    for c0 in range(0, w1_ref.shape[1], th):
        u = jnp.maximum(
            jnp.dot(h, w1_ref[:, c0:c0 + th], preferred_element_type=F32), 0.0)
        y = y + jnp.dot((u * u).astype(BF16), w2_ref[c0:c0 + th, :],
                        preferred_element_type=F32)
    if final_norm:
        y = _rms(y, gf_ref[...])
    o_ref[...] = y


def _post(x, a, w_o, g, w1, w2, g_final, final_norm, tm=512, th=1024):
    t, d = x.shape
    ka = a.shape[1]
    hid = w1.shape[1]
    row = lambda i: (i, 0)
    fixed = lambda i: (0, 0)
    resident = lambda shape: pl.BlockSpec(shape, fixed, pipeline_mode=pl.Buffered(1))
    return pl.pallas_call(
        functools.partial(_post_kernel, final_norm=final_norm, th=th),
        grid=(t // tm,),
        in_specs=[pl.BlockSpec((tm, d), row),
                  pl.BlockSpec((tm, ka), row),
                  resident((ka, d)),
                  resident((1, d)),
                  resident((d, hid)),
                  resident((hid, d)),
                  resident((1, d))],
        out_specs=pl.BlockSpec((tm, d), row),
        out_shape=jax.ShapeDtypeStruct((t, d), F32),
        compiler_params=_params(("parallel",), 56),
        name="post_mlp",
    )(x, a, w_o.astype(BF16), g.reshape(1, d), w1.astype(BF16), w2.astype(BF16),
      g_final.reshape(1, d))


def _mla_mixer(x, g, rope, w_in, q_norm, w_uq, kv_norm, w_ukv, b, s):
    q, kv, kpe = _mla_proj(x, g, w_in, q_norm, w_uq, kv_norm, w_ukv, rope)
    return _mla_attn(q, kv, kpe, b, s)


def _moba_mixer(x, g, w_qkv, bias, b, s):
    n = w_qkv.shape[1]
    (qkv,) = _norm_proj(x, g, w_qkv.astype(BF16), (n,), (BF16,), tm=512)
    return _moba_attn(qkv, bias, b, s)


def _ssd_mixer(x, g, w_in, conv_w, conv_b, dt_bias, a_log, d_skip, norm_w, b, s):
    z, xs, bc, dt = _ssd_proj(x, g, w_in, conv_w, conv_b, s)
    return _ssd_scan(z, xs, bc, dt, dt_bias, a_log, d_skip, norm_w, b, s)


def kernel(x, positions, norm_mix, norm_mlp, norm_final, rel_bias, mla_w_in, mla_q_norm, mla_w_uq, mla_kv_norm, mla_w_ukv, mla_w_o, moba_w_qkv, moba_w_o, ssd_w_in, ssd_conv_w, ssd_conv_b, ssd_dt_bias, ssd_a_log, ssd_d, ssd_norm, ssd_w_out, mlp_w1, mlp_w2):
    b, s, d = x.shape
    xf = x.reshape(b * s, d)
    rope = None
    bias = None
    for i in range(DEPTH):
        kind, j = i % N_MIXERS, i // N_MIXERS
        if kind == 0:
            if rope is None:
                rope = _rope_tables(positions)
            a = _mla_mixer(xf, norm_mix[i], rope, mla_w_in[j], mla_q_norm[j], mla_w_uq[j],
                           mla_kv_norm[j], mla_w_ukv[j], b, s)
            w_o = mla_w_o[j]
        elif kind == 1:
            if bias is None:
                bias = _moba_bias(rel_bias)
            a = _moba_mixer(xf, norm_mix[i], moba_w_qkv[j], bias, b, s)
            w_o = moba_w_o[j]
        else:
            a = _ssd_mixer(xf, norm_mix[i], ssd_w_in[j], ssd_conv_w[j], ssd_conv_b[j],
                           ssd_dt_bias[j], ssd_a_log[j], ssd_d[j], ssd_norm[j], b, s)
            w_o = ssd_w_out[j]
        xf = _post(xf, a, w_o, norm_mlp[i], mlp_w1[i], mlp_w2[i], norm_final,
                   final_norm=(i == DEPTH - 1))
    return xf.reshape(b, s, d)
```

```python
import functools
import math

import numpy as np
import jax
import jax.numpy as jnp
from jax import lax
from jax.experimental import pallas as pl
from jax.experimental.pallas import tpu as pltpu

F32 = jnp.float32
BF16 = jnp.bfloat16
HIGHEST = lax.Precision.HIGHEST

D_MODEL = 1024
DEPTH = 4
N_MIXERS = 3
NORM_EPS = 1e-6
NEG_INF = -1e30

MLA_HEADS = 16
MLA_Q_RANK = 256
MLA_KV_RANK = 256
MLA_NOPE = 128
MLA_ROPE = 64
MLA_V = 128
ROPE_BASE = 10000.0

MOBA_HEADS = 8
MOBA_HEAD_DIM = D_MODEL // MOBA_HEADS
MOBA_BLOCK = 256
MOBA_TOPK = 3
REL_BUCKETS = 32
REL_MAX_DIST = 128

SSD_INNER = 2 * D_MODEL
SSD_HEAD_DIM = 64
SSD_HEADS = SSD_INNER // SSD_HEAD_DIM
SSD_GROUPS = 2
SSD_HEADS_PER_GROUP = SSD_HEADS // SSD_GROUPS
SSD_STATE = 128
SSD_CONV = 4
SSD_CONV_DIM = SSD_INNER + 2 * SSD_GROUPS * SSD_STATE
SSD_CHUNK = 128

MLP_HIDDEN = 4 * D_MODEL

LANES = 128
SUBLANES = 8
MIB = 1024 * 1024

NT_DIMS = (((1,), (1,)), ((), ()))
TN_DIMS = (((0,), (0,)), ((), ()))


def _params(semantics, vmem_mib=48, flags=None):
    return pltpu.CompilerParams(dimension_semantics=semantics,
                                vmem_limit_bytes=vmem_mib * MIB, flags=flags)


def _rms(x, g):
    ms = jnp.mean(x * x, axis=-1, keepdims=True)
    return x * lax.rsqrt(ms + NORM_EPS) * g


def _norm_proj_kernel(x_ref, g_ref, w_ref, *o_refs, splits, col_chunk):
    h = _rms(x_ref[...], g_ref[...]).astype(BF16)
    off = 0
    for o_ref, n in zip(o_refs, splits):
        for c0 in range(0, n, col_chunk):
            c1 = min(c0 + col_chunk, n)
            o_ref[:, c0:c1] = jnp.dot(
                h, w_ref[:, off + c0:off + c1],
                preferred_element_type=F32).astype(o_ref.dtype)
        off += n


def _norm_proj(x, g, w, splits, dtypes, tm):
    t, d = x.shape
    n = w.shape[1]
    assert sum(splits) == n and t % tm == 0
    return pl.pallas_call(
        functools.partial(_norm_proj_kernel, splits=tuple(splits), col_chunk=1024),
        grid=(t // tm,),
        in_specs=[pl.BlockSpec((tm, d), lambda i: (i, 0)),
                  pl.BlockSpec((1, d), lambda i: (0, 0)),
                  pl.BlockSpec((d, n), lambda i: (0, 0))],
        out_specs=[pl.BlockSpec((tm, s), lambda i: (i, 0)) for s in splits],
        out_shape=[jax.ShapeDtypeStruct((t, s), dt) for s, dt in zip(splits, dtypes)],
        compiler_params=_params(("parallel",), 56),
        name="norm_proj",
    )(x, g.reshape(1, d), w)


def _rope_table_kernel(pos_ref, freq_ref, cos_ref, sin_ref):
    ang = pos_ref[...].astype(F32) * freq_ref[...]
    cos_ref[...] = jnp.cos(ang)
    sin_ref[...] = jnp.sin(ang)


def _rope_tables(positions):
    b, s = positions.shape
    t = b * s
    half = MLA_ROPE // 2
    per_row = LANES // half
    rows = t // per_row
    inv_freq = ROPE_BASE ** (-(jnp.arange(0, MLA_ROPE, 2, dtype=F32) / MLA_ROPE))
    pos_rep = jnp.repeat(positions.reshape(rows, per_row), half, axis=1)
    freq = jnp.tile(inv_freq, per_row).reshape(1, LANES)
    tr = 512
    cos, sin = pl.pallas_call(
        _rope_table_kernel,
        grid=(rows // tr,),
        in_specs=[pl.BlockSpec((tr, LANES), lambda i: (i, 0)),
                  pl.BlockSpec((1, LANES), lambda i: (0, 0))],
        out_specs=[pl.BlockSpec((tr, LANES), lambda i: (i, 0))] * 2,
        out_shape=[jax.ShapeDtypeStruct((rows, LANES), F32)] * 2,
        compiler_params=_params(("parallel",)),
        name="rope_tables",
    )(pos_rep, freq)
    cos = cos.reshape(t, half)
    sin = sin.reshape(t, half)
    zeros = lambda n: jnp.zeros((t, n), F32)
    cos_t = jnp.concatenate([cos, cos, zeros(LANES - 2 * half)], axis=1)
    sin_a = jnp.concatenate([-sin, zeros(LANES - half)], axis=1)
    sin_b = jnp.concatenate([zeros(half), sin, zeros(LANES - 2 * half)], axis=1)
    return cos_t, sin_a, sin_b


def _rope128(v, cos_t, sin_a, sin_b):
    half = MLA_ROPE // 2
    return (v * cos_t + pltpu.roll(v, LANES - half, 1) * sin_a
            + pltpu.roll(v, half, 1) * sin_b)


MLA_QW = 2 * LANES
MLA_KVW = MLA_NOPE + MLA_V
MLA_IN_PAD = MLA_Q_RANK + MLA_KV_RANK + LANES


def _mla_proj_kernel(x_ref, g_ref, win_ref, qn_ref, kvn_ref, wuq_ref, wukv_ref,
                     cos_ref, sa_ref, sb_ref, q_ref, kv_ref, kpe_ref):
    h = _rms(x_ref[...], g_ref[...]).astype(BF16)
    proj = jnp.dot(h, win_ref[...], preferred_element_type=F32)
    cq = _rms(proj[:, :MLA_Q_RANK], qn_ref[...]).astype(BF16)
    ckv = _rms(proj[:, MLA_Q_RANK:MLA_Q_RANK + MLA_KV_RANK], kvn_ref[...]).astype(BF16)
    cos_t, sin_a, sin_b = cos_ref[...], sa_ref[...], sb_ref[...]
    kpe_ref[...] = _rope128(proj[:, MLA_Q_RANK + MLA_KV_RANK:], cos_t, sin_a, sin_b).astype(BF16)
    for hh in range(MLA_HEADS):
        c0 = hh * MLA_QW
        qh = jnp.dot(cq, wuq_ref[:, c0:c0 + MLA_QW], preferred_element_type=F32)
        q_ref[:, c0:c0 + MLA_NOPE] = qh[:, :MLA_NOPE].astype(BF16)
        q_ref[:, c0 + MLA_NOPE:c0 + MLA_QW] = _rope128(
            qh[:, MLA_NOPE:], cos_t, sin_a, sin_b).astype(BF16)
    n_kv = MLA_HEADS * MLA_KVW
    for c0 in range(0, n_kv, 1024):
        kv_ref[:, c0:c0 + 1024] = jnp.dot(
            ckv, wukv_ref[:, c0:c0 + 1024], preferred_element_type=F32).astype(BF16)


def _mla_proj(x, g, w_in, q_norm, w_uq, kv_norm, w_ukv, rope, tm=512):
    t, d = x.shape
    nq = MLA_HEADS * MLA_QW
    nkv = MLA_HEADS * MLA_KVW
    w_in_p = jnp.pad(w_in, ((0, 0), (0, MLA_IN_PAD - w_in.shape[1]))).astype(BF16)
    w_uq_p = jnp.pad(w_uq.reshape(MLA_Q_RANK, MLA_HEADS, MLA_NOPE + MLA_ROPE),
                     ((0, 0), (0, 0), (0, MLA_QW - MLA_NOPE - MLA_ROPE))
                     ).reshape(MLA_Q_RANK, nq).astype(BF16)
    w_ukv_p = jnp.swapaxes(w_ukv.reshape(MLA_KV_RANK, MLA_HEADS, 2, MLA_NOPE), 1, 2
                           ).reshape(MLA_KV_RANK, nkv).astype(BF16)
    row = lambda i: (i, 0)
    fixed = lambda i: (0, 0)
    return pl.pallas_call(
        _mla_proj_kernel,
        grid=(t // tm,),
        in_specs=[pl.BlockSpec((tm, d), row),
                  pl.BlockSpec((1, d), fixed),
                  pl.BlockSpec((d, MLA_IN_PAD), fixed),
                  pl.BlockSpec((1, MLA_Q_RANK), fixed),
                  pl.BlockSpec((1, MLA_KV_RANK), fixed),
                  pl.BlockSpec((MLA_Q_RANK, nq), fixed),
                  pl.BlockSpec((MLA_KV_RANK, nkv), fixed),
                  pl.BlockSpec((tm, LANES), row),
                  pl.BlockSpec((tm, LANES), row),
                  pl.BlockSpec((tm, LANES), row)],
        out_specs=[pl.BlockSpec((tm, nq), row),
                   pl.BlockSpec((tm, nkv), row),
                   pl.BlockSpec((tm, LANES), row)],
        out_shape=[jax.ShapeDtypeStruct((t, nq), BF16),
                   jax.ShapeDtypeStruct((t, nkv), BF16),
                   jax.ShapeDtypeStruct((t, LANES), BF16)],
        compiler_params=_params(("parallel",)),
        name="mla_proj",
    )(x, g.reshape(1, d), w_in_p, q_norm.reshape(1, -1), kv_norm.reshape(1, -1),
      w_uq_p, w_ukv_p, *rope)


LOG2E = math.log2(math.e)


def _mla_attn_kernel(q_ref, k_ref, kpe_ref, v_ref, o_ref, s_ref, m_ref,
                     *, hp, tq, nq, scale):
    n = pl.program_id(0)
    c = scale * LOG2E
    kidx = lax.broadcasted_iota(jnp.int32, (tq, tq), 0)
    qidx = lax.broadcasted_iota(jnp.int32, (tq, tq), 1)
    causal = kidx <= qidx

    @pl.when(n == 0)
    def _():
        s_ref[...] = jnp.zeros_like(s_ref)
        m_ref[...] = jnp.zeros_like(m_ref)

    def scores(qi, slot):
        past = qi * tq
        kpe = kpe_ref[0:past + tq, :]
        for hh in range(hp):
            kc = jnp.concatenate(
                [k_ref[0:past + tq, hh * MLA_NOPE:(hh + 1) * MLA_NOPE], kpe], axis=1)
            q = q_ref[:, hh * MLA_QW:(hh + 1) * MLA_QW]
            s_diag = jnp.where(
                causal, lax.dot_general(kc[past:], q, NT_DIMS, preferred_element_type=F32),
                NEG_INF)
            s_ref[slot, hh, past:past + tq, :] = s_diag
            m = jnp.max(s_diag, axis=0, keepdims=True)
            if past:
                s_past = lax.dot_general(kc[:past], q, NT_DIMS, preferred_element_type=F32)
                s_ref[slot, hh, 0:past, :] = s_past
                m = jnp.maximum(m, jnp.max(s_past, axis=0, keepdims=True))
            m_ref[slot, hh] = m

    def attend(qi, slot):
        kv_len = (qi + 1) * tq
        for hh in range(hp):
            p = jnp.exp2((s_ref[slot, hh, 0:kv_len, :] - m_ref[slot, hh]) * c)
            l = jnp.sum(p, axis=0, keepdims=True)
            acc = lax.dot_general(v_ref[0:kv_len, hh * MLA_V:(hh + 1) * MLA_V],
                                  p.astype(BF16), TN_DIMS, preferred_element_type=F32)
            o_ref[:, hh * MLA_V:(hh + 1) * MLA_V] = (acc / l).T.astype(BF16)

    for r in range(nq):
        @pl.when(n % nq == r)
        def _(r=r):
            scores(r, r % 2)
            attend((r - 1) % nq, (r - 1) % 2)


def _mla_attn(q, kv, kpe, b, s, hp=4, tq=256):
    assert MLA_NOPE == MLA_V
    t = b * s
    nq = s // tq
    assert nq % 2 == 0
    groups = MLA_HEADS // hp
    units = b * groups * nq
    scale = (MLA_NOPE + MLA_ROPE) ** -0.5

    def unit(n):
        return n // (groups * nq), (n // nq) % groups, n % nq

    def cur(n):
        return unit(jnp.minimum(n, units - 1))

    def last(n):
        return unit(jnp.maximum(n - 1, 0))

    def q_map(n):
        bb, g, i = cur(n)
        return bb * nq + i, g

    def k_map(n):
        bb, g, _ = cur(n)
        return bb, g

    def kpe_map(n):
        return cur(n)[0], 0

    def v_map(n):
        bb, g, _ = last(n)
        return bb, groups + g

    def o_map(n):
        bb, g, i = last(n)
        return bb * nq + i, g

    return pl.pallas_call(
        functools.partial(_mla_attn_kernel, hp=hp, tq=tq, nq=nq, scale=scale),
        grid=(units + 1,),
        in_specs=[pl.BlockSpec((tq, hp * MLA_QW), q_map),
                  pl.BlockSpec((s, hp * MLA_NOPE), k_map),
                  pl.BlockSpec((s, LANES), kpe_map),
                  pl.BlockSpec((s, hp * MLA_V), v_map)],
        out_specs=pl.BlockSpec((tq, hp * MLA_V), o_map),
        out_shape=jax.ShapeDtypeStruct((t, MLA_HEADS * MLA_V), BF16),
        scratch_shapes=[pltpu.VMEM((2, hp, s, tq), F32),
                        pltpu.VMEM((2, hp, 1, tq), F32)],
        compiler_params=_params(("arbitrary",)),
        name="mla_attn",
    )(q, kv, kpe, kv)


def _t5_bucket_np(n):
    max_exact = REL_BUCKETS // 2
    nf = np.maximum(n, max_exact).astype(np.float32)
    large = max_exact + (np.log(nf / np.float32(max_exact))
                         / np.float32(math.log(REL_MAX_DIST / max_exact))
                         * np.float32(REL_BUCKETS - max_exact)).astype(np.int32)
    large = np.minimum(large, REL_BUCKETS - 1)
    return np.where(n < max_exact, n, large).astype(np.int32)


def _moba_bias_kernel(tab_ref, bucket_ref, o_ref, *, inv_scale):
    h = pl.program_id(0)
    bucket = bucket_ref[...]
    far = tab_ref[REL_BUCKETS - 1, h]
    acc = jnp.zeros(bucket.shape, F32)
    for k in range(REL_BUCKETS - 1):
        acc = jnp.where(bucket == k, tab_ref[k, h] - far, acc)
    o_ref[0] = acc * inv_scale


def _moba_bias(rel_bias):
    blk = MOBA_BLOCK
    key = np.arange(blk)[:, None]
    qry = np.arange(blk)[None, :]
    dist = np.stack([np.maximum(qry - key, 0), blk + qry - key])
    bucket = jnp.asarray(_t5_bucket_np(dist))
    return pl.pallas_call(
        functools.partial(_moba_bias_kernel, inv_scale=MOBA_HEAD_DIM ** 0.5),
        grid=(MOBA_HEADS,),
        in_specs=[pl.BlockSpec(memory_space=pltpu.SMEM),
                  pl.BlockSpec((2, blk, blk), lambda h: (0, 0, 0))],
        out_specs=pl.BlockSpec((1, 2, blk, blk), lambda h: (h, 0, 0, 0)),
        out_shape=jax.ShapeDtypeStruct((MOBA_HEADS, 2, blk, blk), F32),
        compiler_params=_params(("arbitrary",)),
        name="moba_bias",
    )(rel_bias, bucket)


def _moba_attn_kernel(q_ref, k_ref, v_ref, bias_ref, o_ref,
                      sel_ref, s_ref, m_ref, p_ref, *, hp, nb, scale):
    n = pl.program_id(0)
    blk = MOBA_BLOCK
    dh = MOBA_HEAD_DIM
    s_len = nb * blk
    kidx = lax.broadcasted_iota(jnp.int32, (blk, blk), 0)
    qidx = lax.broadcasted_iota(jnp.int32, (blk, blk), 1)
    causal = kidx <= qidx

    @pl.when(n == 0)
    def _():
        s_ref[...] = jnp.zeros_like(s_ref)
        m_ref[...] = jnp.zeros_like(m_ref)

    def select_blocks():
        nidx = lax.broadcasted_iota(jnp.int32, (nb, s_len), 0)
        qblk = lax.broadcasted_iota(jnp.int32, (nb, s_len), 1) // blk
        past = nidx < qblk
        for hh in range(hp):
            cs = slice(hh * dh, (hh + 1) * dh)
            k_mean = jnp.concatenate(
                [jnp.sum(k_ref[n * blk:(n + 1) * blk, cs].astype(F32), axis=0, keepdims=True)
                 for n in range(nb)], axis=0) / blk
            h1 = k_mean.astype(BF16)
            r1 = k_mean - h1.astype(F32)
            h2 = r1.astype(BF16)
            h3 = (r1 - h2.astype(F32)).astype(BF16)
            g3 = lax.dot_general(jnp.concatenate([h1, h2, h3], axis=0), q_ref[:, cs], NT_DIMS,
                                 preferred_element_type=F32)
            gate = g3[0:nb] + g3[nb:2 * nb] + g3[2 * nb:3 * nb]
            gm = jnp.where(past, gate, NEG_INF)
            rank = jnp.zeros((nb, s_len), F32)
            for m in range(nb):
                gm_m = gm[m:m + 1, :]
                tie = jnp.where(gm_m == gm, jnp.where(nidx > m, 1.0, 0.0), 0.0)
                rank = rank + jnp.where(gm_m > gm, 1.0, tie)
            sel = jnp.where(rank < min(MOBA_TOPK, nb), jnp.where(past, 1.0, 0.0), 0.0)
            for qb in range(nb):
                sel_ref[hh, qb] = sel[:, qb * blk:(qb + 1) * blk]

    def scores(qi, slot):
        past = qi * blk
        for hh in range(hp):
            cs = slice(hh * dh, (hh + 1) * dh)
            q = q_ref[past:past + blk, cs]
            s_diag = lax.dot_general(k_ref[past:past + blk, cs], q, NT_DIMS,
                                     preferred_element_type=F32) + bias_ref[hh, 0]
            s_diag = jnp.where(causal, s_diag, NEG_INF)
            s_ref[slot, hh, past:past + blk, :] = s_diag
            m = jnp.max(s_diag, axis=0, keepdims=True)
            if past:
                s_past = lax.dot_general(k_ref[0:past, cs], q, NT_DIMS,
                                         preferred_element_type=F32)
                for j in range(qi):
                    s_j = s_past[j * blk:(j + 1) * blk]
                    if j == qi - 1:
                        s_j = s_j + bias_ref[hh, 1]
                    sel = sel_ref[hh, qi, j:j + 1, :]
                    s_j = jnp.where(sel > 0.5, s_j, NEG_INF)
                    s_ref[slot, hh, j * blk:(j + 1) * blk, :] = s_j
                    m = jnp.maximum(m, jnp.max(s_j, axis=0, keepdims=True))
            m_ref[slot, hh] = m

    def attend(qi, slot):
        kv_len = (qi + 1) * blk
        for hh in range(hp):
            cs = slice(hh * dh, (hh + 1) * dh)
            p = jnp.exp2((s_ref[slot, hh, 0:kv_len, :] - m_ref[slot, hh]) * (scale * LOG2E))
            l = jnp.sum(p, axis=0, keepdims=True)
            p_ref[hh, 0:kv_len, :] = p.astype(BF16)
            acc = lax.dot_general(v_ref[0:kv_len, cs], p_ref[hh, 0:kv_len, :], TN_DIMS,
                                  preferred_element_type=F32)
            o_ref[:, cs] = (acc / l).T.astype(BF16)

    for r in range(nb):
        @pl.when(n % nb == r)
        def _(r=r):
            if r == 0:
                select_blocks()
            scores(r, r % 2)
            attend((r - 1) % nb, (r - 1) % 2)


def _moba_attn(qkv, bias, b, s, hp=4):
    assert MOBA_BLOCK + 1 >= REL_MAX_DIST and s % MOBA_BLOCK == 0
    t = b * s
    blk = MOBA_BLOCK
    nb = s // blk
    assert nb % 2 == 0
    dh = MOBA_HEAD_DIM
    groups = MOBA_HEADS // hp
    units = b * groups * nb

    def unit(n):
        return n // (groups * nb), (n // nb) % groups, n % nb

    def cur(n):
        return unit(jnp.minimum(n, units - 1))

    def prev(n):
        return unit(jnp.maximum(n - 1, 0))

    def q_map(n):
        bb, g, _ = cur(n)
        return bb, g

    def k_map(n):
        bb, g, _ = cur(n)
        return bb, groups + g

    def bias_map(n):
        return cur(n)[1], 0, 0, 0

    def v_map(n):
        bb, g, _ = prev(n)
        return bb, 2 * groups + g

    def o_map(n):
        bb, g, i = prev(n)
        return bb * nb + i, g

    return pl.pallas_call(
        functools.partial(_moba_attn_kernel, hp=hp, nb=nb, scale=dh ** -0.5),
        grid=(units + 1,),
        in_specs=[pl.BlockSpec((s, hp * dh), q_map),
                  pl.BlockSpec((s, hp * dh), k_map),
                  pl.BlockSpec((s, hp * dh), v_map),
                  pl.BlockSpec((hp, 2, blk, blk), bias_map)],
        out_specs=pl.BlockSpec((blk, hp * dh), o_map),
        out_shape=jax.ShapeDtypeStruct((t, MOBA_HEADS * dh), BF16),
        scratch_shapes=[pltpu.VMEM((hp, nb, nb, blk), F32),
                        pltpu.VMEM((2, hp, s, blk), F32),
                        pltpu.VMEM((2, hp, 1, blk), F32),
                        pltpu.VMEM((hp, s, blk), BF16)],
        compiler_params=_params(("arbitrary",)),
        name="moba_attn",
    )(qkv, qkv, qkv, bias)


SSD_GN = SSD_GROUPS * SSD_STATE
SSD_GW = SSD_INNER // SSD_GROUPS


def _softplus(x):
    return jnp.maximum(x, 0.0) + jnp.log1p(jnp.exp(-jnp.abs(x)))


def _silu(x):
    return x * jax.nn.sigmoid(x)


def _expand_heads(v, e3):
    h1 = v.astype(BF16)
    r1 = v - h1.astype(F32)
    h2 = r1.astype(BF16)
    h3 = (r1 - h2.astype(F32)).astype(BF16)
    return jnp.dot(jnp.concatenate([h1, h2, h3], axis=1), e3, preferred_element_type=F32)


def _ssd_proj_kernel(x_ref, g_ref, w_ref, cw_ref, cb_ref, z_ref, xs_ref, bc_ref, dt_ref,
                     xpad_ref, *, tiles_per_seq):
    i = pl.program_id(0)
    tm = x_ref.shape[0]
    halo = SUBLANES
    c_xbc = SSD_INNER
    c_dt = SSD_INNER + SSD_CONV_DIM

    @pl.when(i % tiles_per_seq == 0)
    def _():
        xpad_ref[0:halo, :] = jnp.zeros((halo, SSD_CONV_DIM), F32)

    h = _rms(x_ref[...], g_ref[...]).astype(BF16)
    dt_ref[...] = jnp.dot(h, w_ref[:, c_dt:c_dt + LANES], preferred_element_type=F32)
    for c0 in range(0, SSD_CONV_DIM, 512):
        cs = slice(c0, c0 + 512)
        if c0 < SSD_INNER:
            z_ref[:, cs] = jnp.dot(h, w_ref[:, cs], preferred_element_type=F32)
        raw = jnp.dot(h, w_ref[:, c_xbc + c0:c_xbc + c0 + 512], preferred_element_type=F32)
        xpad_ref[halo:halo + tm, cs] = raw
        conv = cb_ref[:, cs] + cw_ref[SSD_CONV - 1:SSD_CONV, cs] * raw
        for k in range(SSD_CONV - 1):
            r0 = halo - (SSD_CONV - 1) + k
            conv = conv + cw_ref[k:k + 1, cs] * xpad_ref[r0:r0 + tm, cs]
        xpad_ref[0:halo, cs] = xpad_ref[tm:tm + halo, cs]
        xc = _silu(conv)
        if c0 < SSD_INNER:
            xs_ref[:, cs] = xc
        else:
            bc_ref[:, c0 - SSD_INNER:c0 - SSD_INNER + 512] = xc.astype(BF16)


def _ssd_proj(x, g, w_in, conv_w, conv_b, s, tm=512):
    t, d = x.shape
    assert SSD_CONV_DIM % 512 == 0 and SSD_INNER % 1024 == 0 and s % tm == 0
    w_pad = jnp.pad(w_in, ((0, 0), (0, LANES - SSD_HEADS))).astype(BF16)
    n = w_pad.shape[1]
    row = lambda i: (i, 0)
    fixed = lambda i: (0, 0)
    resident = lambda shape: pl.BlockSpec(shape, fixed, pipeline_mode=pl.Buffered(1))
    return pl.pallas_call(
        functools.partial(_ssd_proj_kernel, tiles_per_seq=s // tm),
        grid=(t // tm,),
        in_specs=[pl.BlockSpec((tm, d), row),
                  resident((1, d)),
                  resident((d, n)),
                  resident((SSD_CONV, SSD_CONV_DIM)),
                  resident((1, SSD_CONV_DIM))],
        out_specs=[pl.BlockSpec((tm, SSD_INNER), row),
                   pl.BlockSpec((tm, SSD_INNER), row),
                   pl.BlockSpec((tm, 2 * SSD_GN), row),
                   pl.BlockSpec((tm, LANES), row)],
        out_shape=[jax.ShapeDtypeStruct((t, SSD_INNER), F32),
                   jax.ShapeDtypeStruct((t, SSD_INNER), F32),
                   jax.ShapeDtypeStruct((t, 2 * SSD_GN), BF16),
                   jax.ShapeDtypeStruct((t, LANES), F32)],
        scratch_shapes=[pltpu.VMEM((tm + SUBLANES, SSD_CONV_DIM), F32)],
        compiler_params=_params(("arbitrary",), 56),
        name="ssd_proj",
    )(x, g.reshape(1, d), w_pad, conv_w, conv_b.reshape(1, -1))


def _ssd_kernel(z_ref, xs_ref, bc_ref, dt_ref, dtb_ref, alog_ref, dexp_ref,
                nw_ref, e3_ref, o_ref, st_ref, *, chunks):
    @pl.when(pl.program_id(1) == 0)
    def _():
        st_ref[...] = jnp.zeros_like(st_ref)

    for cc in range(chunks):
        rows = slice(cc * SSD_CHUNK, (cc + 1) * SSD_CHUNK)
        _ssd_chunk(z_ref.at[rows], xs_ref.at[rows], bc_ref.at[rows], dt_ref.at[rows],
                   dtb_ref, alog_ref, dexp_ref, nw_ref, e3_ref, o_ref.at[rows], st_ref)


def _ssd_chunk(z_ref, xs_ref, bc_ref, dt_ref, dtb_ref, alog_ref, dexp_ref,
               nw_ref, e3_ref, o_ref, st_ref):
    L = SSD_CHUNK
    P = SSD_HEAD_DIM
    N = SSD_STATE
    xs = xs_ref[...]
    bm = bc_ref[:, :SSD_GN]
    cm = bc_ref[:, SSD_GN:]
    xs_b = xs.astype(BF16)

    dt = _softplus(dt_ref[...] + dtb_ref[...])
    a = -jnp.exp(alog_ref[...])
    row = lax.broadcasted_iota(jnp.int32, (L, L), 0)
    col = lax.broadcasted_iota(jnp.int32, (L, L), 1)
    tril = row >= col
    a_cs = jnp.dot(jnp.where(tril, 1.0, 0.0).astype(F32), dt * a,
                   precision=HIGHEST, preferred_element_type=F32)
    a_cs_t = a_cs.T
    dt_t = dt.T
    e3 = e3_ref[...]
    decay_exp = _expand_heads(jnp.exp(a_cs), e3)
    w_exp = _expand_heads(jnp.exp(a_cs[L - 1:L, :] - a_cs) * dt, e3)
    xw = (xs * w_exp).astype(BF16)
    lane = lax.broadcasted_iota(jnp.int32, (L, LANES), 1)

    ys = []
    for g in range(SSD_GROUPS):
        gs = slice(g * SSD_GW, (g + 1) * SSD_GW)
        b_g = bm[:, g * N:(g + 1) * N]
        c_g = cm[:, g * N:(g + 1) * N]
        cb = lax.dot_general(c_g, b_g, NT_DIMS, preferred_element_type=F32)
        state = st_ref[:, gs]
        y_off = jnp.dot(c_g, state.astype(BF16), preferred_element_type=F32) * decay_exp[:, gs]
        new_state = lax.dot_general(b_g, xw[:, gs], TN_DIMS, preferred_element_type=F32)
        st_ref[:, gs] = state * decay_exp[L - 1:L, gs] + new_state
        y_diag = []
        for pr in range(SSD_HEADS_PER_GROUP // 2):
            h0 = g * SSD_HEADS_PER_GROUP + 2 * pr
            mats = []
            for hr in (h0, h0 + 1):
                seg = a_cs[:, hr:hr + 1] - a_cs_t[hr:hr + 1, :]
                decay = jnp.exp(jnp.where(tril, seg, NEG_INF))
                mats.append((cb * decay * dt_t[hr:hr + 1, :]).astype(BF16))
            xp = xs_b[:, h0 * P:(h0 + 2) * P]
            zero = jnp.zeros_like(xp)
            x2 = jnp.concatenate([jnp.where(lane < P, xp, zero),
                                  jnp.where(lane >= P, xp, zero)], axis=0)
            y_diag.append(jnp.dot(jnp.concatenate(mats, axis=1), x2,
                                  preferred_element_type=F32))
        ys.append(jnp.concatenate(y_diag, axis=1) + y_off)
    y = jnp.concatenate(ys, axis=1) + dexp_ref[...] * xs
    gated = y * _silu(z_ref[...])
    outs = []
    for g in range(SSD_GROUPS):
        gg = gated[:, g * SSD_GW:(g + 1) * SSD_GW]
        ms = jnp.mean(gg * gg, axis=-1, keepdims=True)
        outs.append(gg * lax.rsqrt(ms + NORM_EPS))
    o_ref[...] = (jnp.concatenate(outs, axis=1) * nw_ref[...]).astype(BF16)


def _ssd_scan(z, xs, bc, dt, dt_bias, a_log, d_skip, norm_w, b, s, chunks=4):
    t = b * s
    L = chunks * SSD_CHUNK
    assert s % L == 0
    nc = s // L
    pad_h = lambda v: jnp.pad(v, (0, LANES - SSD_HEADS)).reshape(1, LANES)
    e = np.zeros((LANES, SSD_INNER), np.float32)
    for r in range(SSD_HEADS):
        e[r, r * SSD_HEAD_DIM:(r + 1) * SSD_HEAD_DIM] = 1.0
    e3 = jnp.asarray(np.concatenate([e, e, e], axis=0), dtype=BF16)
    d_exp = jnp.repeat(d_skip, SSD_HEAD_DIM).reshape(1, SSD_INNER)
    row = lambda bb, c: (bb * nc + c, 0)
    fixed = lambda bb, c: (0, 0)
    return pl.pallas_call(
        functools.partial(_ssd_kernel, chunks=chunks),
        grid=(b, nc),
        in_specs=[pl.BlockSpec((L, SSD_INNER), row),
                  pl.BlockSpec((L, SSD_INNER), row),
                  pl.BlockSpec((L, 2 * SSD_GN), row),
                  pl.BlockSpec((L, LANES), row),
                  pl.BlockSpec((1, LANES), fixed),
                  pl.BlockSpec((1, LANES), fixed),
                  pl.BlockSpec((1, SSD_INNER), fixed),
                  pl.BlockSpec((1, SSD_INNER), fixed),
                  pl.BlockSpec((3 * LANES, SSD_INNER), fixed)],
        out_specs=pl.BlockSpec((L, SSD_INNER), row),
        out_shape=jax.ShapeDtypeStruct((t, SSD_INNER), BF16),
        scratch_shapes=[pltpu.VMEM((SSD_STATE, SSD_INNER), F32)],
        compiler_params=_params(("parallel", "arbitrary")),
        name="ssd_scan",
    )(z, xs, bc, dt, pad_h(dt_bias), pad_h(a_log), d_exp, norm_w.reshape(1, -1), e3)


def _post_kernel(x_ref, a_ref, wo_ref, g_ref, w1_ref, w2_ref, gf_ref, o_ref,
                 *, final_norm, th):
    x1 = x_ref[...] + jnp.dot(a_ref[...], wo_ref[...], preferred_element_type=F32)
    h = _rms(x1, g_ref[...]).astype(BF16)
    y = x1
    for c0 in range(0, w1_ref.shape[1], th):
        u = jnp.maximum(
            jnp.dot(h, w1_ref[:, c0:c0 + th], preferred_element_type=F32), 0.0)
        y = y + jnp.dot((u * u).astype(BF16), w2_ref[c0:c0 + th, :],
                        preferred_element_type=F32)
    if final_norm:
        y = _rms(y, gf_ref[...])
    o_ref[...] = y


def _post(x, a, w_o, g, w1, w2, g_final, final_norm, tm=512, th=1024):
    t, d = x.shape
    ka = a.shape[1]
    hid = w1.shape[1]
    row = lambda i: (i, 0)
    fixed = lambda i: (0, 0)
    resident = lambda shape: pl.BlockSpec(shape, fixed, pipeline_mode=pl.Buffered(1))
    return pl.pallas_call(
        functools.partial(_post_kernel, final_norm=final_norm, th=th),
        grid=(t // tm,),
        in_specs=[pl.BlockSpec((tm, d), row),
                  pl.BlockSpec((tm, ka), row),
                  resident((ka, d)),
                  resident((1, d)),
                  resident((d, hid)),
                  resident((hid, d)),
                  resident((1, d))],
        out_specs=pl.BlockSpec((tm, d), row),
        out_shape=jax.ShapeDtypeStruct((t, d), F32),
        compiler_params=_params(("parallel",), 56),
        name="post_mlp",
    )(x, a, w_o.astype(BF16), g.reshape(1, d), w1.astype(BF16), w2.astype(BF16),
      g_final.reshape(1, d))


def _mla_mixer(x, g, rope, w_in, q_norm, w_uq, kv_norm, w_ukv, b, s):
    q, kv, kpe = _mla_proj(x, g, w_in, q_norm, w_uq, kv_norm, w_ukv, rope)
    return _mla_attn(q, kv, kpe, b, s)


def _moba_mixer(x, g, w_qkv, bias, b, s):
    n = w_qkv.shape[1]
    (qkv,) = _norm_proj(x, g, w_qkv.astype(BF16), (n,), (BF16,), tm=512)
    return _moba_attn(qkv, bias, b, s)


def _ssd_mixer(x, g, w_in, conv_w, conv_b, dt_bias, a_log, d_skip, norm_w, b, s):
    z, xs, bc, dt = _ssd_proj(x, g, w_in, conv_w, conv_b, s)
    return _ssd_scan(z, xs, bc, dt, dt_bias, a_log, d_skip, norm_w, b, s)


def kernel(x, positions, norm_mix, norm_mlp, norm_final, rel_bias, mla_w_in, mla_q_norm, mla_w_uq, mla_kv_norm, mla_w_ukv, mla_w_o, moba_w_qkv, moba_w_o, ssd_w_in, ssd_conv_w, ssd_conv_b, ssd_dt_bias, ssd_a_log, ssd_d, ssd_norm, ssd_w_out, mlp_w1, mlp_w2):
    b, s, d = x.shape
    xf = x.reshape(b * s, d)
    rope = None
    bias = None
    for i in range(DEPTH):
        kind, j = i % N_MIXERS, i // N_MIXERS
        if kind == 0:
            if rope is None:
                rope = _rope_tables(positions)
            a = _mla_mixer(xf, norm_mix[i], rope, mla_w_in[j], mla_q_norm[j], mla_w_uq[j],
                           mla_kv_norm[j], mla_w_ukv[j], b, s)
            w_o = mla_w_o[j]
        elif kind == 1:
            if bias is None:
                bias = _moba_bias(rel_bias)
            a = _moba_mixer(xf, norm_mix[i], moba_w_qkv[j], bias, b, s)
            w_o = moba_w_o[j]
        else:
            a = _ssd_mixer(xf, norm_mix[i], ssd_w_in[j], ssd_conv_w[j], ssd_conv_b[j],
                           ssd_dt_bias[j], ssd_a_log[j], ssd_d[j], ssd_norm[j], b, s)
            w_o = ssd_w_out[j]
        xf = _post(xf, a, w_o, norm_mlp[i], mlp_w1[i], mlp_w2[i], norm_final,
                   final_norm=(i == DEPTH - 1))
    return xf.reshape(b, s, d)
```

```python
import functools
import math

import numpy as np
import jax
import jax.numpy as jnp
from jax import lax
from jax.experimental import pallas as pl
from jax.experimental.pallas import tpu as pltpu

F32 = jnp.float32
BF16 = jnp.bfloat16
HIGHEST = lax.Precision.HIGHEST

D_MODEL = 1024
DEPTH = 4
N_MIXERS = 3
NORM_EPS = 1e-6
NEG_INF = -1e30

MLA_HEADS = 16
MLA_Q_RANK = 256
MLA_KV_RANK = 256
MLA_NOPE = 128
MLA_ROPE = 64
MLA_V = 128
ROPE_BASE = 10000.0

MOBA_HEADS = 8
MOBA_HEAD_DIM = D_MODEL // MOBA_HEADS
MOBA_BLOCK = 256
MOBA_TOPK = 3
REL_BUCKETS = 32
REL_MAX_DIST = 128

SSD_INNER = 2 * D_MODEL
SSD_HEAD_DIM = 64
SSD_HEADS = SSD_INNER // SSD_HEAD_DIM
SSD_GROUPS = 2
SSD_HEADS_PER_GROUP = SSD_HEADS // SSD_GROUPS
SSD_STATE = 128
SSD_CONV = 4
SSD_CONV_DIM = SSD_INNER + 2 * SSD_GROUPS * SSD_STATE
SSD_CHUNK = 128

MLP_HIDDEN = 4 * D_MODEL

LANES = 128
SUBLANES = 8
MIB = 1024 * 1024

NT_DIMS = (((1,), (1,)), ((), ()))
TN_DIMS = (((0,), (0,)), ((), ()))


def _params(semantics, vmem_mib=48, flags=None):
    return pltpu.CompilerParams(dimension_semantics=semantics,
                                vmem_limit_bytes=vmem_mib * MIB, flags=flags)


def _rms(x, g):
    ms = jnp.mean(x * x, axis=-1, keepdims=True)
    return x * lax.rsqrt(ms + NORM_EPS) * g


def _norm_proj_kernel(x_ref, g_ref, w_ref, *o_refs, splits, col_chunk):
    h = _rms(x_ref[...], g_ref[...]).astype(BF16)
    off = 0
    for o_ref, n in zip(o_refs, splits):
        for c0 in range(0, n, col_chunk):
            c1 = min(c0 + col_chunk, n)
            o_ref[:, c0:c1] = jnp.dot(
                h, w_ref[:, off + c0:off + c1],
                preferred_element_type=F32).astype(o_ref.dtype)
        off += n


def _norm_proj(x, g, w, splits, dtypes, tm):
    t, d = x.shape
    n = w.shape[1]
    assert sum(splits) == n and t % tm == 0
    return pl.pallas_call(
        functools.partial(_norm_proj_kernel, splits=tuple(splits), col_chunk=1024),
        grid=(t // tm,),
        in_specs=[pl.BlockSpec((tm, d), lambda i: (i, 0)),
                  pl.BlockSpec((1, d), lambda i: (0, 0)),
                  pl.BlockSpec((d, n), lambda i: (0, 0))],
        out_specs=[pl.BlockSpec((tm, s), lambda i: (i, 0)) for s in splits],
        out_shape=[jax.ShapeDtypeStruct((t, s), dt) for s, dt in zip(splits, dtypes)],
        compiler_params=_params(("parallel",), 56),
        name="norm_proj",
    )(x, g.reshape(1, d), w)


def _rope_table_kernel(pos_ref, freq_ref, cos_ref, sin_ref):
    ang = pos_ref[...].astype(F32) * freq_ref[...]
    cos_ref[...] = jnp.cos(ang)
    sin_ref[...] = jnp.sin(ang)


def _rope_tables(positions):
    b, s = positions.shape
    t = b * s
    half = MLA_ROPE // 2
    per_row = LANES // half
    rows = t // per_row
    inv_freq = ROPE_BASE ** (-(jnp.arange(0, MLA_ROPE, 2, dtype=F32) / MLA_ROPE))
    pos_rep = jnp.repeat(positions.reshape(rows, per_row), half, axis=1)
    freq = jnp.tile(inv_freq, per_row).reshape(1, LANES)
    tr = 512
    cos, sin = pl.pallas_call(
        _rope_table_kernel,
        grid=(rows // tr,),
        in_specs=[pl.BlockSpec((tr, LANES), lambda i: (i, 0)),
                  pl.BlockSpec((1, LANES), lambda i: (0, 0))],
        out_specs=[pl.BlockSpec((tr, LANES), lambda i: (i, 0))] * 2,
        out_shape=[jax.ShapeDtypeStruct((rows, LANES), F32)] * 2,
        compiler_params=_params(("parallel",)),
        name="rope_tables",
    )(pos_rep, freq)
    cos = cos.reshape(t, half)
    sin = sin.reshape(t, half)
    zeros = lambda n: jnp.zeros((t, n), F32)
    cos_t = jnp.concatenate([cos, cos, zeros(LANES - 2 * half)], axis=1)
    sin_a = jnp.concatenate([-sin, zeros(LANES - half)], axis=1)
    sin_b = jnp.concatenate([zeros(half), sin, zeros(LANES - 2 * half)], axis=1)
    return cos_t, sin_a, sin_b


def _rope128(v, cos_t, sin_a, sin_b):
    half = MLA_ROPE // 2
    return (v * cos_t + pltpu.roll(v, LANES - half, 1) * sin_a
            + pltpu.roll(v, half, 1) * sin_b)


MLA_QW = 2 * LANES
MLA_KVW = MLA_NOPE + MLA_V
MLA_IN_PAD = MLA_Q_RANK + MLA_KV_RANK + LANES


def _mla_proj_kernel(x_ref, g_ref, win_ref, qn_ref, kvn_ref, wuq_ref, wukv_ref,
                     cos_ref, sa_ref, sb_ref, q_ref, kv_ref, kpe_ref):
    h = _rms(x_ref[...], g_ref[...]).astype(BF16)
    proj = jnp.dot(h, win_ref[...], preferred_element_type=F32)
    cq = _rms(proj[:, :MLA_Q_RANK], qn_ref[...]).astype(BF16)
    ckv = _rms(proj[:, MLA_Q_RANK:MLA_Q_RANK + MLA_KV_RANK], kvn_ref[...]).astype(BF16)
    cos_t, sin_a, sin_b = cos_ref[...], sa_ref[...], sb_ref[...]
    kpe_ref[...] = _rope128(proj[:, MLA_Q_RANK + MLA_KV_RANK:], cos_t, sin_a, sin_b).astype(BF16)
    for hh in range(MLA_HEADS):
        c0 = hh * MLA_QW
        qh = jnp.dot(cq, wuq_ref[:, c0:c0 + MLA_QW], preferred_element_type=F32)
        q_ref[:, c0:c0 + MLA_NOPE] = qh[:, :MLA_NOPE].astype(BF16)
        q_ref[:, c0 + MLA_NOPE:c0 + MLA_QW] = _rope128(
            qh[:, MLA_NOPE:], cos_t, sin_a, sin_b).astype(BF16)
    n_kv = MLA_HEADS * MLA_KVW
    for c0 in range(0, n_kv, 1024):
        kv_ref[:, c0:c0 + 1024] = jnp.dot(
            ckv, wukv_ref[:, c0:c0 + 1024], preferred_element_type=F32).astype(BF16)


def _mla_proj(x, g, w_in, q_norm, w_uq, kv_norm, w_ukv, rope, tm=512):
    t, d = x.shape
    nq = MLA_HEADS * MLA_QW
    nkv = MLA_HEADS * MLA_KVW
    w_in_p = jnp.pad(w_in, ((0, 0), (0, MLA_IN_PAD - w_in.shape[1]))).astype(BF16)
    w_uq_p = jnp.pad(w_uq.reshape(MLA_Q_RANK, MLA_HEADS, MLA_NOPE + MLA_ROPE),
                     ((0, 0), (0, 0), (0, MLA_QW - MLA_NOPE - MLA_ROPE))
                     ).reshape(MLA_Q_RANK, nq).astype(BF16)
    w_ukv_p = jnp.swapaxes(w_ukv.reshape(MLA_KV_RANK, MLA_HEADS, 2, MLA_NOPE), 1, 2
                           ).reshape(MLA_KV_RANK, nkv).astype(BF16)
    row = lambda i: (i, 0)
    fixed = lambda i: (0, 0)
    return pl.pallas_call(
        _mla_proj_kernel,
        grid=(t // tm,),
        in_specs=[pl.BlockSpec((tm, d), row),
                  pl.BlockSpec((1, d), fixed),
                  pl.BlockSpec((d, MLA_IN_PAD), fixed),
                  pl.BlockSpec((1, MLA_Q_RANK), fixed),
                  pl.BlockSpec((1, MLA_KV_RANK), fixed),
                  pl.BlockSpec((MLA_Q_RANK, nq), fixed),
                  pl.BlockSpec((MLA_KV_RANK, nkv), fixed),
                  pl.BlockSpec((tm, LANES), row),
                  pl.BlockSpec((tm, LANES), row),
                  pl.BlockSpec((tm, LANES), row)],
        out_specs=[pl.BlockSpec((tm, nq), row),
                   pl.BlockSpec((tm, nkv), row),
                   pl.BlockSpec((tm, LANES), row)],
        out_shape=[jax.ShapeDtypeStruct((t, nq), BF16),
                   jax.ShapeDtypeStruct((t, nkv), BF16),
                   jax.ShapeDtypeStruct((t, LANES), BF16)],
        compiler_params=_params(("parallel",)),
        name="mla_proj",
    )(x, g.reshape(1, d), w_in_p, q_norm.reshape(1, -1), kv_norm.reshape(1, -1),
      w_uq_p, w_ukv_p, *rope)


LOG2E = math.log2(math.e)


def _mla_attn_kernel(q_ref, k_ref, kpe_ref, v_ref, o_ref, s_ref, m_ref,
                     *, hp, tq, nq, scale):
    n = pl.program_id(0)
    c = scale * LOG2E
    kidx = lax.broadcasted_iota(jnp.int32, (tq, tq), 0)
    qidx = lax.broadcasted_iota(jnp.int32, (tq, tq), 1)
    causal = kidx <= qidx

    @pl.when(n == 0)
    def _():
        s_ref[...] = jnp.zeros_like(s_ref)
        m_ref[...] = jnp.zeros_like(m_ref)

    def scores(qi, slot):
        past = qi * tq
        kpe = kpe_ref[0:past + tq, :]
        for hh in range(hp):
            kc = jnp.concatenate(
                [k_ref[0:past + tq, hh * MLA_NOPE:(hh + 1) * MLA_NOPE], kpe], axis=1)
            q = q_ref[:, hh * MLA_QW:(hh + 1) * MLA_QW]
            s_diag = jnp.where(
                causal, lax.dot_general(kc[past:], q, NT_DIMS, preferred_element_type=F32),
                NEG_INF)
            s_ref[slot, hh, past:past + tq, :] = s_diag
            m = jnp.max(s_diag, axis=0, keepdims=True)
            if past:
                s_past = lax.dot_general(kc[:past], q, NT_DIMS, preferred_element_type=F32)
                s_ref[slot, hh, 0:past, :] = s_past
                m = jnp.maximum(m, jnp.max(s_past, axis=0, keepdims=True))
            m_ref[slot, hh] = m

    def attend(qi, slot):
        kv_len = (qi + 1) * tq
        for hh in range(hp):
            p = jnp.exp2((s_ref[slot, hh, 0:kv_len, :] - m_ref[slot, hh]) * c)
            l = jnp.sum(p, axis=0, keepdims=True)
            acc = lax.dot_general(v_ref[0:kv_len, hh * MLA_V:(hh + 1) * MLA_V],
                                  p.astype(BF16), TN_DIMS, preferred_element_type=F32)
            o_ref[:, hh * MLA_V:(hh + 1) * MLA_V] = (acc / l).T.astype(BF16)

    for r in range(nq):
        @pl.when(n % nq == r)
        def _(r=r):
            scores(r, r % 2)
            attend((r - 1) % nq, (r - 1) % 2)


def _mla_attn(q, kv, kpe, b, s, hp=4, tq=256):
    assert MLA_NOPE == MLA_V
    t = b * s
    nq = s // tq
    assert nq % 2 == 0
    groups = MLA_HEADS // hp
    units = b * groups * nq
    scale = (MLA_NOPE + MLA_ROPE) ** -0.5

    def unit(n):
        return n // (groups * nq), (n // nq) % groups, n % nq

    def cur(n):
        return unit(jnp.minimum(n, units - 1))

    def last(n):
        return unit(jnp.maximum(n - 1, 0))

    def q_map(n):
        bb, g, i = cur(n)
        return bb * nq + i, g

    def k_map(n):
        bb, g, _ = cur(n)
        return bb, g

    def kpe_map(n):
        return cur(n)[0], 0

    def v_map(n):
        bb, g, _ = last(n)
        return bb, groups + g

    def o_map(n):
        bb, g, i = last(n)
        return bb * nq + i, g

    return pl.pallas_call(
        functools.partial(_mla_attn_kernel, hp=hp, tq=tq, nq=nq, scale=scale),
        grid=(units + 1,),
        in_specs=[pl.BlockSpec((tq, hp * MLA_QW), q_map),
                  pl.BlockSpec((s, hp * MLA_NOPE), k_map),
                  pl.BlockSpec((s, LANES), kpe_map),
                  pl.BlockSpec((s, hp * MLA_V), v_map)],
        out_specs=pl.BlockSpec((tq, hp * MLA_V), o_map),
        out_shape=jax.ShapeDtypeStruct((t, MLA_HEADS * MLA_V), BF16),
        scratch_shapes=[pltpu.VMEM((2, hp, s, tq), F32),
                        pltpu.VMEM((2, hp, 1, tq), F32)],
        compiler_params=_params(("arbitrary",)),
        name="mla_attn",
    )(q, kv, kpe, kv)


def _t5_bucket_np(n):
    max_exact = REL_BUCKETS // 2
    nf = np.maximum(n, max_exact).astype(np.float32)
    large = max_exact + (np.log(nf / np.float32(max_exact))
                         / np.float32(math.log(REL_MAX_DIST / max_exact))
                         * np.float32(REL_BUCKETS - max_exact)).astype(np.int32)
    large = np.minimum(large, REL_BUCKETS - 1)
    return np.where(n < max_exact, n, large).astype(np.int32)


def _moba_bias_kernel(tab_ref, bucket_ref, o_ref, *, inv_scale):
    h = pl.program_id(0)
    bucket = bucket_ref[...]
    far = tab_ref[REL_BUCKETS - 1, h]
    acc = jnp.zeros(bucket.shape, F32)
    for k in range(REL_BUCKETS - 1):
        acc = jnp.where(bucket == k, tab_ref[k, h] - far, acc)
    o_ref[0] = acc * inv_scale


def _moba_bias(rel_bias):
    blk = MOBA_BLOCK
    key = np.arange(blk)[:, None]
    qry = np.arange(blk)[None, :]
    dist = np.stack([np.maximum(qry - key, 0), blk + qry - key])
    bucket = jnp.asarray(_t5_bucket_np(dist))
    return pl.pallas_call(
        functools.partial(_moba_bias_kernel, inv_scale=MOBA_HEAD_DIM ** 0.5),
        grid=(MOBA_HEADS,),
        in_specs=[pl.BlockSpec(memory_space=pltpu.SMEM),
                  pl.BlockSpec((2, blk, blk), lambda h: (0, 0, 0))],
        out_specs=pl.BlockSpec((1, 2, blk, blk), lambda h: (h, 0, 0, 0)),
        out_shape=jax.ShapeDtypeStruct((MOBA_HEADS, 2, blk, blk), F32),
        compiler_params=_params(("arbitrary",)),
        name="moba_bias",
    )(rel_bias, bucket)


def _moba_attn_kernel(q_ref, k_ref, v_ref, bias_ref, o_ref,
                      sel_ref, s_ref, m_ref, p_ref, *, hp, nb, scale):
    n = pl.program_id(0)
    blk = MOBA_BLOCK
    dh = MOBA_HEAD_DIM
    s_len = nb * blk
    kidx = lax.broadcasted_iota(jnp.int32, (blk, blk), 0)
    qidx = lax.broadcasted_iota(jnp.int32, (blk, blk), 1)
    causal = kidx <= qidx

    @pl.when(n == 0)
    def _():
        s_ref[...] = jnp.zeros_like(s_ref)
        m_ref[...] = jnp.zeros_like(m_ref)

    def select_blocks():
        nidx = lax.broadcasted_iota(jnp.int32, (nb, s_len), 0)
        qblk = lax.broadcasted_iota(jnp.int32, (nb, s_len), 1) // blk
        past = nidx < qblk
        for hh in range(hp):
            cs = slice(hh * dh, (hh + 1) * dh)
            k_mean = jnp.concatenate(
                [jnp.sum(k_ref[n * blk:(n + 1) * blk, cs].astype(F32), axis=0, keepdims=True)
                 for n in range(nb)], axis=0) / blk
            h1 = k_mean.astype(BF16)
            r1 = k_mean - h1.astype(F32)
            h2 = r1.astype(BF16)
            h3 = (r1 - h2.astype(F32)).astype(BF16)
            g3 = lax.dot_general(jnp.concatenate([h1, h2, h3], axis=0), q_ref[:, cs], NT_DIMS,
                                 preferred_element_type=F32)
            gate = g3[0:nb] + g3[nb:2 * nb] + g3[2 * nb:3 * nb]
            gm = jnp.where(past, gate, NEG_INF)
            rank = jnp.zeros((nb, s_len), F32)
            for m in range(nb):
                gm_m = gm[m:m + 1, :]
                tie = jnp.where(gm_m == gm, jnp.where(nidx > m, 1.0, 0.0), 0.0)
                rank = rank + jnp.where(gm_m > gm, 1.0, tie)
            sel = jnp.where(rank < min(MOBA_TOPK, nb), jnp.where(past, 1.0, 0.0), 0.0)
            for qb in range(nb):
                sel_ref[hh, qb] = sel[:, qb * blk:(qb + 1) * blk]

    def scores(qi, slot):
        past = qi * blk
        for hh in range(hp):
            cs = slice(hh * dh, (hh + 1) * dh)
            q = q_ref[past:past + blk, cs]
            s_diag = lax.dot_general(k_ref[past:past + blk, cs], q, NT_DIMS,
                                     preferred_element_type=F32) + bias_ref[hh, 0]
            s_diag = jnp.where(causal, s_diag, NEG_INF)
            s_ref[slot, hh, past:past + blk, :] = s_diag
            m = jnp.max(s_diag, axis=0, keepdims=True)
            if past:
                s_past = lax.dot_general(k_ref[0:past, cs], q, NT_DIMS,
                                         preferred_element_type=F32)
                for j in range(qi):
                    s_j = s_past[j * blk:(j + 1) * blk]
                    if j == qi - 1:
                        s_j = s_j + bias_ref[hh, 1]
                    sel = sel_ref[hh, qi, j:j + 1, :]
                    s_j = jnp.where(sel > 0.5, s_j, NEG_INF)
                    s_ref[slot, hh, j * blk:(j + 1) * blk, :] = s_j
                    m = jnp.maximum(m, jnp.max(s_j, axis=0, keepdims=True))
            m_ref[slot, hh] = m

    def attend(qi, slot):
        kv_len = (qi + 1) * blk
        for hh in range(hp):
            cs = slice(hh * dh, (hh + 1) * dh)
            p = jnp.exp2((s_ref[slot, hh, 0:kv_len, :] - m_ref[slot, hh]) * (scale * LOG2E))
            l = jnp.sum(p, axis=0, keepdims=True)
            p_ref[hh, 0:kv_len, :] = p.astype(BF16)
            acc = lax.dot_general(v_ref[0:kv_len, cs], p_ref[hh, 0:kv_len, :], TN_DIMS,
                                  preferred_element_type=F32)
            o_ref[:, cs] = (acc / l).T.astype(BF16)

    for r in range(nb):
        @pl.when(n % nb == r)
        def _(r=r):
            if r == 0:
                select_blocks()
            scores(r, r % 2)
            attend((r - 1) % nb, (r - 1) % 2)


def _moba_attn(qkv, bias, b, s, hp=4):
    assert MOBA_BLOCK + 1 >= REL_MAX_DIST and s % MOBA_BLOCK == 0
    t = b * s
    blk = MOBA_BLOCK
    nb = s // blk
    assert nb % 2 == 0
    dh = MOBA_HEAD_DIM
    groups = MOBA_HEADS // hp
    units = b * groups * nb

    def unit(n):
        return n // (groups * nb), (n // nb) % groups, n % nb

    def cur(n):
        return unit(jnp.minimum(n, units - 1))

    def prev(n):
        return unit(jnp.maximum(n - 1, 0))

    def q_map(n):
        bb, g, _ = cur(n)
        return bb, g

    def k_map(n):
        bb, g, _ = cur(n)
        return bb, groups + g

    def bias_map(n):
        return cur(n)[1], 0, 0, 0

    def v_map(n):
        bb, g, _ = prev(n)
        return bb, 2 * groups + g

    def o_map(n):
        bb, g, i = prev(n)
        return bb * nb + i, g

    return pl.pallas_call(
        functools.partial(_moba_attn_kernel, hp=hp, nb=nb, scale=dh ** -0.5),
        grid=(units + 1,),
        in_specs=[pl.BlockSpec((s, hp * dh), q_map),
                  pl.BlockSpec((s, hp * dh), k_map),
                  pl.BlockSpec((s, hp * dh), v_map),
                  pl.BlockSpec((hp, 2, blk, blk), bias_map)],
        out_specs=pl.BlockSpec((blk, hp * dh), o_map),
        out_shape=jax.ShapeDtypeStruct((t, MOBA_HEADS * dh), BF16),
        scratch_shapes=[pltpu.VMEM((hp, nb, nb, blk), F32),
                        pltpu.VMEM((2, hp, s, blk), F32),
                        pltpu.VMEM((2, hp, 1, blk), F32),
                        pltpu.VMEM((hp, s, blk), BF16)],
        compiler_params=_params(("arbitrary",)),
        name="moba_attn",
    )(qkv, qkv, qkv, bias)


SSD_GN = SSD_GROUPS * SSD_STATE
SSD_GW = SSD_INNER // SSD_GROUPS


def _softplus(x):
    return jnp.maximum(x, 0.0) + jnp.log1p(jnp.exp(-jnp.abs(x)))


def _silu(x):
    return x * jax.nn.sigmoid(x)


def _expand_heads(v, e3):
    h1 = v.astype(BF16)
    r1 = v - h1.astype(F32)
    h2 = r1.astype(BF16)
    h3 = (r1 - h2.astype(F32)).astype(BF16)
    return jnp.dot(jnp.concatenate([h1, h2, h3], axis=1), e3, preferred_element_type=F32)


def _ssd_proj_kernel(x_ref, g_ref, w_ref, cw_ref, cb_ref, z_ref, xs_ref, bc_ref, dt_ref,
                     xpad_ref, *, tiles_per_seq):
    i = pl.program_id(0)
    tm = x_ref.shape[0]
    halo = SUBLANES
    c_xbc = SSD_INNER
    c_dt = SSD_INNER + SSD_CONV_DIM

    @pl.when(i % tiles_per_seq == 0)
    def _():
        xpad_ref[0:halo, :] = jnp.zeros((halo, SSD_CONV_DIM), F32)

    h = _rms(x_ref[...], g_ref[...]).astype(BF16)
    dt_ref[...] = jnp.dot(h, w_ref[:, c_dt:c_dt + LANES], preferred_element_type=F32)
    for c0 in range(0, SSD_CONV_DIM, 512):
        cs = slice(c0, c0 + 512)
        if c0 < SSD_INNER:
            z_ref[:, cs] = jnp.dot(h, w_ref[:, cs], preferred_element_type=F32)
        raw = jnp.dot(h, w_ref[:, c_xbc + c0:c_xbc + c0 + 512], preferred_element_type=F32)
        xpad_ref[halo:halo + tm, cs] = raw
        conv = cb_ref[:, cs] + cw_ref[SSD_CONV - 1:SSD_CONV, cs] * raw
        for k in range(SSD_CONV - 1):
            r0 = halo - (SSD_CONV - 1) + k
            conv = conv + cw_ref[k:k + 1, cs] * xpad_ref[r0:r0 + tm, cs]
        xpad_ref[0:halo, cs] = xpad_ref[tm:tm + halo, cs]
        xc = _silu(conv)
        if c0 < SSD_INNER:
            xs_ref[:, cs] = xc
        else:
            bc_ref[:, c0 - SSD_INNER:c0 - SSD_INNER + 512] = xc.astype(BF16)


def _ssd_proj(x, g, w_in, conv_w, conv_b, s, tm=512):
    t, d = x.shape
    assert SSD_CONV_DIM % 512 == 0 and SSD_INNER % 1024 == 0 and s % tm == 0
    w_pad = jnp.pad(w_in, ((0, 0), (0, LANES - SSD_HEADS))).astype(BF16)
    n = w_pad.shape[1]
    row = lambda i: (i, 0)
    fixed = lambda i: (0, 0)
    resident = lambda shape: pl.BlockSpec(shape, fixed, pipeline_mode=pl.Buffered(1))
    return pl.pallas_call(
        functools.partial(_ssd_proj_kernel, tiles_per_seq=s // tm),
        grid=(t // tm,),
        in_specs=[pl.BlockSpec((tm, d), row),
                  resident((1, d)),
                  resident((d, n)),
                  resident((SSD_CONV, SSD_CONV_DIM)),
                  resident((1, SSD_CONV_DIM))],
        out_specs=[pl.BlockSpec((tm, SSD_INNER), row),
                   pl.BlockSpec((tm, SSD_INNER), row),
                   pl.BlockSpec((tm, 2 * SSD_GN), row),
                   pl.BlockSpec((tm, LANES), row)],
        out_shape=[jax.ShapeDtypeStruct((t, SSD_INNER), F32),
                   jax.ShapeDtypeStruct((t, SSD_INNER), F32),
                   jax.ShapeDtypeStruct((t, 2 * SSD_GN), BF16),
                   jax.ShapeDtypeStruct((t, LANES), F32)],
        scratch_shapes=[pltpu.VMEM((tm + SUBLANES, SSD_CONV_DIM), F32)],
        compiler_params=_params(("arbitrary",), 56),
        name="ssd_proj",
    )(x, g.reshape(1, d), w_pad, conv_w, conv_b.reshape(1, -1))


def _ssd_kernel(z_ref, xs_ref, bc_ref, dt_ref, dtb_ref, alog_ref, dexp_ref,
                nw_ref, e3_ref, o_ref, st_ref, *, chunks):
    @pl.when(pl.program_id(1) == 0)
    def _():
        st_ref[...] = jnp.zeros_like(st_ref)

    for cc in range(chunks):
        rows = slice(cc * SSD_CHUNK, (cc + 1) * SSD_CHUNK)
        _ssd_chunk(z_ref.at[rows], xs_ref.at[rows], bc_ref.at[rows], dt_ref.at[rows],
                   dtb_ref, alog_ref, dexp_ref, nw_ref, e3_ref, o_ref.at[rows], st_ref)


def _ssd_chunk(z_ref, xs_ref, bc_ref, dt_ref, dtb_ref, alog_ref, dexp_ref,
               nw_ref, e3_ref, o_ref, st_ref):
    L = SSD_CHUNK
    P = SSD_HEAD_DIM
    N = SSD_STATE
    xs = xs_ref[...]
    bm = bc_ref[:, :SSD_GN]
    cm = bc_ref[:, SSD_GN:]
    xs_b = xs.astype(BF16)

    dt = _softplus(dt_ref[...] + dtb_ref[...])
    a = -jnp.exp(alog_ref[...])
    row = lax.broadcasted_iota(jnp.int32, (L, L), 0)
    col = lax.broadcasted_iota(jnp.int32, (L, L), 1)
    tril = row >= col
    a_cs = jnp.dot(jnp.where(tril, 1.0, 0.0).astype(F32), dt * a,
                   precision=HIGHEST, preferred_element_type=F32)
    a_cs_t = a_cs.T
    dt_t = dt.T
    e3 = e3_ref[...]
    decay_exp = _expand_heads(jnp.exp(a_cs), e3)
    w_exp = _expand_heads(jnp.exp(a_cs[L - 1:L, :] - a_cs) * dt, e3)
    xw = (xs * w_exp).astype(BF16)
    lane = lax.broadcasted_iota(jnp.int32, (L, LANES), 1)

    ys = []
    for g in range(SSD_GROUPS):
        gs = slice(g * SSD_GW, (g + 1) * SSD_GW)
        b_g = bm[:, g * N:(g + 1) * N]
        c_g = cm[:, g * N:(g + 1) * N]
        cb = lax.dot_general(c_g, b_g, NT_DIMS, preferred_element_type=F32)
        state = st_ref[:, gs]
        y_off = jnp.dot(c_g, state.astype(BF16), preferred_element_type=F32) * decay_exp[:, gs]
        new_state = lax.dot_general(b_g, xw[:, gs], TN_DIMS, preferred_element_type=F32)
        st_ref[:, gs] = state * decay_exp[L - 1:L, gs] + new_state
        y_diag = []
        for pr in range(SSD_HEADS_PER_GROUP // 2):
            h0 = g * SSD_HEADS_PER_GROUP + 2 * pr
            mats = []
            for hr in (h0, h0 + 1):
                seg = a_cs[:, hr:hr + 1] - a_cs_t[hr:hr + 1, :]
                decay = jnp.exp(jnp.where(tril, seg, NEG_INF))
                mats.append((cb * decay * dt_t[hr:hr + 1, :]).astype(BF16))
            xp = xs_b[:, h0 * P:(h0 + 2) * P]
            zero = jnp.zeros_like(xp)
            x2 = jnp.concatenate([jnp.where(lane < P, xp, zero),
                                  jnp.where(lane >= P, xp, zero)], axis=0)
            y_diag.append(jnp.dot(jnp.concatenate(mats, axis=1), x2,
                                  preferred_element_type=F32))
        ys.append(jnp.concatenate(y_diag, axis=1) + y_off)
    y = jnp.concatenate(ys, axis=1) + dexp_ref[...] * xs
    gated = y * _silu(z_ref[...])
    outs = []
    for g in range(SSD_GROUPS):
        gg = gated[:, g * SSD_GW:(g + 1) * SSD_GW]
        ms = jnp.mean(gg * gg, axis=-1, keepdims=True)
        outs.append(gg * lax.rsqrt(ms + NORM_EPS))
    o_ref[...] = (jnp.concatenate(outs, axis=1) * nw_ref[...]).astype(BF16)


def _ssd_scan(z, xs, bc, dt, dt_bias, a_log, d_skip, norm_w, b, s, chunks=4):
    t = b * s
    L = chunks * SSD_CHUNK
    assert s % L == 0
    nc = s // L
    pad_h = lambda v: jnp.pad(v, (0, LANES - SSD_HEADS)).reshape(1, LANES)
    e = np.zeros((LANES, SSD_INNER), np.float32)
    for r in range(SSD_HEADS):
        e[r, r * SSD_HEAD_DIM:(r + 1) * SSD_HEAD_DIM] = 1.0
    e3 = jnp.asarray(np.concatenate([e, e, e], axis=0), dtype=BF16)
    d_exp = jnp.repeat(d_skip, SSD_HEAD_DIM).reshape(1, SSD_INNER)
    row = lambda bb, c: (bb * nc + c, 0)
    fixed = lambda bb, c: (0, 0)
    return pl.pallas_call(
        functools.partial(_ssd_kernel, chunks=chunks),
        grid=(b, nc),
        in_specs=[pl.BlockSpec((L, SSD_INNER), row),
                  pl.BlockSpec((L, SSD_INNER), row),
                  pl.BlockSpec((L, 2 * SSD_GN), row),
                  pl.BlockSpec((L, LANES), row),
                  pl.BlockSpec((1, LANES), fixed),
                  pl.BlockSpec((1, LANES), fixed),
                  pl.BlockSpec((1, SSD_INNER), fixed),
                  pl.BlockSpec((1, SSD_INNER), fixed),
                  pl.BlockSpec((3 * LANES, SSD_INNER), fixed)],
        out_specs=pl.BlockSpec((L, SSD_INNER), row),
        out_shape=jax.ShapeDtypeStruct((t, SSD_INNER), BF16),
        scratch_shapes=[pltpu.VMEM((SSD_STATE, SSD_INNER), F32)],
        compiler_params=_params(("parallel", "arbitrary")),
        name="ssd_scan",
    )(z, xs, bc, dt, pad_h(dt_bias), pad_h(a_log), d_exp, norm_w.reshape(1, -1), e3)


def _post_kernel(x_ref, a_ref, wo_ref, g_ref, w1_ref, w2_ref, gf_ref, o_ref,
                 *, final_norm, th):
    x1 = x_ref[...] + jnp.dot(a_ref[...], wo_ref[...], preferred_element_type=F32)
    h = _rms(x1, g_ref[...]).astype(BF16)
    y = x1
    for c0 in range(0, w1_ref.shape[1], th):
        u = jnp.maximum(
            jnp.dot(h, w1_ref[:, c0:c0 + th], preferred_element_type=F32), 0.0)
        y = y + jnp.dot((u * u).astype(BF16), w2_ref[c0:c0 + th, :],
                        preferred_element_type=F32)
    if final_norm:
        y = _rms(y, gf_ref[...])
    o_ref[...] = y


def _post(x, a, w_o, g, w1_all, w2_all, layer, g_final, final_norm, tm=512, th=1024):
    t, d = x.shape
    ka = a.shape[1]
    hid = w1_all.shape[2]
    row = lambda i: (i, 0)
    fixed = lambda i: (0, 0)
    resident = lambda shape: pl.BlockSpec(shape, fixed, pipeline_mode=pl.Buffered(1))
    of_layer = lambda shape: pl.BlockSpec((None,) + shape, lambda i: (layer, 0, 0),
                                          pipeline_mode=pl.Buffered(1))
    return pl.pallas_call(
        functools.partial(_post_kernel, final_norm=final_norm, th=th),
        grid=(t // tm,),
        in_specs=[pl.BlockSpec((tm, d), row),
                  pl.BlockSpec((tm, ka), row),
                  resident((ka, d)),
                  resident((1, d)),
                  of_layer((d, hid)),
                  of_layer((hid, d)),
                  resident((1, d))],
        out_specs=pl.BlockSpec((tm, d), row),
        out_shape=jax.ShapeDtypeStruct((t, d), F32),
        compiler_params=_params(("parallel",), 56),
        name="post_mlp",
    )(x, a, w_o.astype(BF16), g.reshape(1, d), w1_all, w2_all, g_final.reshape(1, d))


def _mla_mixer(x, g, rope, w_in, q_norm, w_uq, kv_norm, w_ukv, b, s):
    q, kv, kpe = _mla_proj(x, g, w_in, q_norm, w_uq, kv_norm, w_ukv, rope)
    return _mla_attn(q, kv, kpe, b, s)


def _moba_mixer(x, g, w_qkv, bias, b, s):
    n = w_qkv.shape[1]
    (qkv,) = _norm_proj(x, g, w_qkv.astype(BF16), (n,), (BF16,), tm=1024)
    return _moba_attn(qkv, bias, b, s)


def _ssd_mixer(x, g, w_in, conv_w, conv_b, dt_bias, a_log, d_skip, norm_w, b, s):
    z, xs, bc, dt = _ssd_proj(x, g, w_in, conv_w, conv_b, s)
    return _ssd_scan(z, xs, bc, dt, dt_bias, a_log, d_skip, norm_w, b, s)


def kernel(x, positions, norm_mix, norm_mlp, norm_final, rel_bias, mla_w_in, mla_q_norm, mla_w_uq, mla_kv_norm, mla_w_ukv, mla_w_o, moba_w_qkv, moba_w_o, ssd_w_in, ssd_conv_w, ssd_conv_b, ssd_dt_bias, ssd_a_log, ssd_d, ssd_norm, ssd_w_out, mlp_w1, mlp_w2):
    b, s, d = x.shape
    xf = x.reshape(b * s, d)
    rope = None
    bias = None
    w1_all = mlp_w1.astype(BF16)
    w2_all = mlp_w2.astype(BF16)
    for i in range(DEPTH):
        kind, j = i % N_MIXERS, i // N_MIXERS
        if kind == 0:
            if rope is None:
                rope = _rope_tables(positions)
            a = _mla_mixer(xf, norm_mix[i], rope, mla_w_in[j], mla_q_norm[j], mla_w_uq[j],
                           mla_kv_norm[j], mla_w_ukv[j], b, s)
            w_o = mla_w_o[j]
        elif kind == 1:
            if bias is None:
                bias = _moba_bias(rel_bias)
            a = _moba_mixer(xf, norm_mix[i], moba_w_qkv[j], bias, b, s)
            w_o = moba_w_o[j]
        else:
            a = _ssd_mixer(xf, norm_mix[i], ssd_w_in[j], ssd_conv_w[j], ssd_conv_b[j],
                           ssd_dt_bias[j], ssd_a_log[j], ssd_d[j], ssd_norm[j], b, s)
            w_o = ssd_w_out[j]
        xf = _post(xf, a, w_o, norm_mlp[i], w1_all, w2_all, i, norm_final,
                   final_norm=(i == DEPTH - 1))
    return xf.reshape(b, s, d)
```

```python
import functools
import math

import numpy as np
import jax
import jax.numpy as jnp
from jax import lax
from jax.experimental import pallas as pl
from jax.experimental.pallas import tpu as pltpu

F32 = jnp.float32
BF16 = jnp.bfloat16
HIGHEST = lax.Precision.HIGHEST

D_MODEL = 1024
DEPTH = 4
N_MIXERS = 3
NORM_EPS = 1e-6
NEG_INF = -1e30

MLA_HEADS = 16
MLA_Q_RANK = 256
MLA_KV_RANK = 256
MLA_NOPE = 128
MLA_ROPE = 64
MLA_V = 128
ROPE_BASE = 10000.0

MOBA_HEADS = 8
MOBA_HEAD_DIM = D_MODEL // MOBA_HEADS
MOBA_BLOCK = 256
MOBA_TOPK = 3
REL_BUCKETS = 32
REL_MAX_DIST = 128

SSD_INNER = 2 * D_MODEL
SSD_HEAD_DIM = 64
SSD_HEADS = SSD_INNER // SSD_HEAD_DIM
SSD_GROUPS = 2
SSD_HEADS_PER_GROUP = SSD_HEADS // SSD_GROUPS
SSD_STATE = 128
SSD_CONV = 4
SSD_CONV_DIM = SSD_INNER + 2 * SSD_GROUPS * SSD_STATE
SSD_CHUNK = 128

MLP_HIDDEN = 4 * D_MODEL

LANES = 128
SUBLANES = 8
MIB = 1024 * 1024

NT_DIMS = (((1,), (1,)), ((), ()))
TN_DIMS = (((0,), (0,)), ((), ()))


def _params(semantics, vmem_mib=48, flags=None):
    return pltpu.CompilerParams(dimension_semantics=semantics,
                                vmem_limit_bytes=vmem_mib * MIB, flags=flags)


def _rms(x, g):
    ms = jnp.mean(x * x, axis=-1, keepdims=True)
    return x * lax.rsqrt(ms + NORM_EPS) * g


def _norm_proj_kernel(x_ref, g_ref, w_ref, *o_refs, splits, col_chunk):
    h = _rms(x_ref[...], g_ref[...]).astype(BF16)
    off = 0
    for o_ref, n in zip(o_refs, splits):
        for c0 in range(0, n, col_chunk):
            c1 = min(c0 + col_chunk, n)
            o_ref[:, c0:c1] = jnp.dot(
                h, w_ref[:, off + c0:off + c1],
                preferred_element_type=F32).astype(o_ref.dtype)
        off += n


def _norm_proj(x, g, w, splits, dtypes, tm):
    t, d = x.shape
    n = w.shape[1]
    assert sum(splits) == n and t % tm == 0
    return pl.pallas_call(
        functools.partial(_norm_proj_kernel, splits=tuple(splits), col_chunk=1024),
        grid=(t // tm,),
        in_specs=[pl.BlockSpec((tm, d), lambda i: (i, 0)),
                  pl.BlockSpec((1, d), lambda i: (0, 0)),
                  pl.BlockSpec((d, n), lambda i: (0, 0))],
        out_specs=[pl.BlockSpec((tm, s), lambda i: (i, 0)) for s in splits],
        out_shape=[jax.ShapeDtypeStruct((t, s), dt) for s, dt in zip(splits, dtypes)],
        compiler_params=_params(("parallel",), 56),
        name="norm_proj",
    )(x, g.reshape(1, d), w)


def _rope_table_kernel(pos_ref, freq_ref, cos_ref, sin_ref):
    ang = pos_ref[...].astype(F32) * freq_ref[...]
    cos_ref[...] = jnp.cos(ang)
    sin_ref[...] = jnp.sin(ang)


def _rope_tables(positions):
    b, s = positions.shape
    t = b * s
    half = MLA_ROPE // 2
    per_row = LANES // half
    rows = t // per_row
    inv_freq = ROPE_BASE ** (-(jnp.arange(0, MLA_ROPE, 2, dtype=F32) / MLA_ROPE))
    pos_rep = jnp.repeat(positions.reshape(rows, per_row), half, axis=1)
    freq = jnp.tile(inv_freq, per_row).reshape(1, LANES)
    tr = 512
    cos, sin = pl.pallas_call(
        _rope_table_kernel,
        grid=(rows // tr,),
        in_specs=[pl.BlockSpec((tr, LANES), lambda i: (i, 0)),
                  pl.BlockSpec((1, LANES), lambda i: (0, 0))],
        out_specs=[pl.BlockSpec((tr, LANES), lambda i: (i, 0))] * 2,
        out_shape=[jax.ShapeDtypeStruct((rows, LANES), F32)] * 2,
        compiler_params=_params(("parallel",)),
        name="rope_tables",
    )(pos_rep, freq)
    cos = cos.reshape(t, half)
    sin = sin.reshape(t, half)
    zeros = lambda n: jnp.zeros((t, n), F32)
    cos_t = jnp.concatenate([cos, cos, zeros(LANES - 2 * half)], axis=1)
    sin_a = jnp.concatenate([-sin, zeros(LANES - half)], axis=1)
    sin_b = jnp.concatenate([zeros(half), sin, zeros(LANES - 2 * half)], axis=1)
    return cos_t, sin_a, sin_b


def _rope128(v, cos_t, sin_a, sin_b):
    half = MLA_ROPE // 2
    return (v * cos_t + pltpu.roll(v, LANES - half, 1) * sin_a
            + pltpu.roll(v, half, 1) * sin_b)


MLA_QW = 2 * LANES
MLA_KVW = MLA_NOPE + MLA_V
MLA_IN_PAD = MLA_Q_RANK + MLA_KV_RANK + LANES


def _mla_proj_kernel(x_ref, g_ref, win_ref, qn_ref, kvn_ref, wuq_ref, wukv_ref,
                     cos_ref, sa_ref, sb_ref, q_ref, kv_ref, kpe_ref):
    h = _rms(x_ref[...], g_ref[...]).astype(BF16)
    proj = jnp.dot(h, win_ref[...], preferred_element_type=F32)
    cq = _rms(proj[:, :MLA_Q_RANK], qn_ref[...]).astype(BF16)
    ckv = _rms(proj[:, MLA_Q_RANK:MLA_Q_RANK + MLA_KV_RANK], kvn_ref[...]).astype(BF16)
    cos_t, sin_a, sin_b = cos_ref[...], sa_ref[...], sb_ref[...]
    kpe_ref[...] = _rope128(proj[:, MLA_Q_RANK + MLA_KV_RANK:], cos_t, sin_a, sin_b).astype(BF16)
    for hh in range(MLA_HEADS):
        c0 = hh * MLA_QW
        qh = jnp.dot(cq, wuq_ref[:, c0:c0 + MLA_QW], preferred_element_type=F32)
        q_ref[:, c0:c0 + MLA_NOPE] = qh[:, :MLA_NOPE].astype(BF16)
        q_ref[:, c0 + MLA_NOPE:c0 + MLA_QW] = _rope128(
            qh[:, MLA_NOPE:], cos_t, sin_a, sin_b).astype(BF16)
    n_kv = MLA_HEADS * MLA_KVW
    for c0 in range(0, n_kv, 1024):
        kv_ref[:, c0:c0 + 1024] = jnp.dot(
            ckv, wukv_ref[:, c0:c0 + 1024], preferred_element_type=F32).astype(BF16)


def _mla_proj(x, g, w_in, q_norm, w_uq, kv_norm, w_ukv, rope, tm=512):
    t, d = x.shape
    nq = MLA_HEADS * MLA_QW
    nkv = MLA_HEADS * MLA_KVW
    w_in_p = jnp.pad(w_in, ((0, 0), (0, MLA_IN_PAD - w_in.shape[1]))).astype(BF16)
    w_uq_p = jnp.pad(w_uq.reshape(MLA_Q_RANK, MLA_HEADS, MLA_NOPE + MLA_ROPE),
                     ((0, 0), (0, 0), (0, MLA_QW - MLA_NOPE - MLA_ROPE))
                     ).reshape(MLA_Q_RANK, nq).astype(BF16)
    w_ukv_p = jnp.swapaxes(w_ukv.reshape(MLA_KV_RANK, MLA_HEADS, 2, MLA_NOPE), 1, 2
                           ).reshape(MLA_KV_RANK, nkv).astype(BF16)
    row = lambda i: (i, 0)
    fixed = lambda i: (0, 0)
    return pl.pallas_call(
        _mla_proj_kernel,
        grid=(t // tm,),
        in_specs=[pl.BlockSpec((tm, d), row),
                  pl.BlockSpec((1, d), fixed),
                  pl.BlockSpec((d, MLA_IN_PAD), fixed),
                  pl.BlockSpec((1, MLA_Q_RANK), fixed),
                  pl.BlockSpec((1, MLA_KV_RANK), fixed),
                  pl.BlockSpec((MLA_Q_RANK, nq), fixed),
                  pl.BlockSpec((MLA_KV_RANK, nkv), fixed),
                  pl.BlockSpec((tm, LANES), row),
                  pl.BlockSpec((tm, LANES), row),
                  pl.BlockSpec((tm, LANES), row)],
        out_specs=[pl.BlockSpec((tm, nq), row),
                   pl.BlockSpec((tm, nkv), row),
                   pl.BlockSpec((tm, LANES), row)],
        out_shape=[jax.ShapeDtypeStruct((t, nq), BF16),
                   jax.ShapeDtypeStruct((t, nkv), BF16),
                   jax.ShapeDtypeStruct((t, LANES), BF16)],
        compiler_params=_params(("parallel",)),
        name="mla_proj",
    )(x, g.reshape(1, d), w_in_p, q_norm.reshape(1, -1), kv_norm.reshape(1, -1),
      w_uq_p, w_ukv_p, *rope)


LOG2E = math.log2(math.e)


def _mla_attn_kernel(q_ref, k_ref, kpe_ref, v_ref, o_ref, s_ref, m_ref,
                     *, hp, tq, nq, scale):
    n = pl.program_id(0)
    c = scale * LOG2E
    kidx = lax.broadcasted_iota(jnp.int32, (tq, tq), 0)
    qidx = lax.broadcasted_iota(jnp.int32, (tq, tq), 1)
    causal = kidx <= qidx

    @pl.when(n == 0)
    def _():
        s_ref[...] = jnp.zeros_like(s_ref)
        m_ref[...] = jnp.zeros_like(m_ref)

    def scores(qi, slot):
        past = qi * tq
        kpe = kpe_ref[0:past + tq, :]
        for hh in range(hp):
            kc = jnp.concatenate(
                [k_ref[0:past + tq, hh * MLA_NOPE:(hh + 1) * MLA_NOPE], kpe], axis=1)
            q = q_ref[:, hh * MLA_QW:(hh + 1) * MLA_QW]
            s_diag = jnp.where(
                causal, lax.dot_general(kc[past:], q, NT_DIMS, preferred_element_type=F32),
                NEG_INF)
            s_ref[slot, hh, past:past + tq, :] = s_diag
            m = jnp.max(s_diag, axis=0, keepdims=True)
            if past:
                s_past = lax.dot_general(kc[:past], q, NT_DIMS, preferred_element_type=F32)
                s_ref[slot, hh, 0:past, :] = s_past
                m = jnp.maximum(m, jnp.max(s_past, axis=0, keepdims=True))
            m_ref[slot, hh] = m

    def attend(qi, slot):
        kv_len = (qi + 1) * tq
        for hh in range(hp):
            p = jnp.exp2((s_ref[slot, hh, 0:kv_len, :] - m_ref[slot, hh]) * c)
            l = jnp.sum(p, axis=0, keepdims=True)
            acc = lax.dot_general(v_ref[0:kv_len, hh * MLA_V:(hh + 1) * MLA_V],
                                  p.astype(BF16), TN_DIMS, preferred_element_type=F32)
            o_ref[:, hh * MLA_V:(hh + 1) * MLA_V] = (acc / l).T.astype(BF16)

    for r in range(nq):
        @pl.when(n % nq == r)
        def _(r=r):
            scores(r, r % 2)
            attend((r - 1) % nq, (r - 1) % 2)


def _mla_attn(q, kv, kpe, b, s, hp=4, tq=512):
    assert MLA_NOPE == MLA_V
    t = b * s
    nq = s // tq
    assert nq % 2 == 0
    groups = MLA_HEADS // hp
    units = b * groups * nq
    scale = (MLA_NOPE + MLA_ROPE) ** -0.5

    def unit(n):
        return n // (groups * nq), (n // nq) % groups, n % nq

    def cur(n):
        return unit(jnp.minimum(n, units - 1))

    def last(n):
        return unit(jnp.maximum(n - 1, 0))

    def q_map(n):
        bb, g, i = cur(n)
        return bb * nq + i, g

    def k_map(n):
        bb, g, _ = cur(n)
        return bb, g

    def kpe_map(n):
        return cur(n)[0], 0

    def v_map(n):
        bb, g, _ = last(n)
        return bb, groups + g

    def o_map(n):
        bb, g, i = last(n)
        return bb * nq + i, g

    return pl.pallas_call(
        functools.partial(_mla_attn_kernel, hp=hp, tq=tq, nq=nq, scale=scale),
        grid=(units + 1,),
        in_specs=[pl.BlockSpec((tq, hp * MLA_QW), q_map),
                  pl.BlockSpec((s, hp * MLA_NOPE), k_map),
                  pl.BlockSpec((s, LANES), kpe_map),
                  pl.BlockSpec((s, hp * MLA_V), v_map)],
        out_specs=pl.BlockSpec((tq, hp * MLA_V), o_map),
        out_shape=jax.ShapeDtypeStruct((t, MLA_HEADS * MLA_V), BF16),
        scratch_shapes=[pltpu.VMEM((2, hp, s, tq), F32),
                        pltpu.VMEM((2, hp, 1, tq), F32)],
        compiler_params=_params(("arbitrary",), 56),
        name="mla_attn",
    )(q, kv, kpe, kv)


def _t5_bucket_np(n):
    max_exact = REL_BUCKETS // 2
    nf = np.maximum(n, max_exact).astype(np.float32)
    large = max_exact + (np.log(nf / np.float32(max_exact))
                         / np.float32(math.log(REL_MAX_DIST / max_exact))
                         * np.float32(REL_BUCKETS - max_exact)).astype(np.int32)
    large = np.minimum(large, REL_BUCKETS - 1)
    return np.where(n < max_exact, n, large).astype(np.int32)


def _moba_bias_kernel(tab_ref, bucket_ref, o_ref, *, inv_scale):
    h = pl.program_id(0)
    bucket = bucket_ref[...]
    far = tab_ref[REL_BUCKETS - 1, h]
    acc = jnp.zeros(bucket.shape, F32)
    for k in range(REL_BUCKETS - 1):
        acc = jnp.where(bucket == k, tab_ref[k, h] - far, acc)
    o_ref[0] = acc * inv_scale


def _moba_bias(rel_bias):
    blk = MOBA_BLOCK
    key = np.arange(blk)[:, None]
    qry = np.arange(blk)[None, :]
    dist = np.stack([np.maximum(qry - key, 0), blk + qry - key])
    bucket = jnp.asarray(_t5_bucket_np(dist))
    return pl.pallas_call(
        functools.partial(_moba_bias_kernel, inv_scale=MOBA_HEAD_DIM ** 0.5),
        grid=(MOBA_HEADS,),
        in_specs=[pl.BlockSpec(memory_space=pltpu.SMEM),
                  pl.BlockSpec((2, blk, blk), lambda h: (0, 0, 0))],
        out_specs=pl.BlockSpec((1, 2, blk, blk), lambda h: (h, 0, 0, 0)),
        out_shape=jax.ShapeDtypeStruct((MOBA_HEADS, 2, blk, blk), F32),
        compiler_params=_params(("arbitrary",)),
        name="moba_bias",
    )(rel_bias, bucket)


def _moba_attn_kernel(q_ref, k_ref, v_ref, bias_ref, o_ref,
                      sel_ref, s_ref, m_ref, p_ref, *, hp, nb, scale):
    n = pl.program_id(0)
    blk = MOBA_BLOCK
    dh = MOBA_HEAD_DIM
    s_len = nb * blk
    kidx = lax.broadcasted_iota(jnp.int32, (blk, blk), 0)
    qidx = lax.broadcasted_iota(jnp.int32, (blk, blk), 1)
    causal = kidx <= qidx

    @pl.when(n == 0)
    def _():
        s_ref[...] = jnp.zeros_like(s_ref)
        m_ref[...] = jnp.zeros_like(m_ref)

    def select_blocks():
        nidx = lax.broadcasted_iota(jnp.int32, (nb, s_len), 0)
        qblk = lax.broadcasted_iota(jnp.int32, (nb, s_len), 1) // blk
        past = nidx < qblk
        for hh in range(hp):
            cs = slice(hh * dh, (hh + 1) * dh)
            k_mean = jnp.concatenate(
                [jnp.sum(k_ref[n * blk:(n + 1) * blk, cs].astype(F32), axis=0, keepdims=True)
                 for n in range(nb)], axis=0) / blk
            h1 = k_mean.astype(BF16)
            r1 = k_mean - h1.astype(F32)
            h2 = r1.astype(BF16)
            h3 = (r1 - h2.astype(F32)).astype(BF16)
            g3 = lax.dot_general(jnp.concatenate([h1, h2, h3], axis=0), q_ref[:, cs], NT_DIMS,
                                 preferred_element_type=F32)
            gate = g3[0:nb] + g3[nb:2 * nb] + g3[2 * nb:3 * nb]
            gm = jnp.where(past, gate, NEG_INF)
            rank = jnp.zeros((nb, s_len), F32)
            for m in range(nb):
                gm_m = gm[m:m + 1, :]
                tie = jnp.where(gm_m == gm, jnp.where(nidx > m, 1.0, 0.0), 0.0)
                rank = rank + jnp.where(gm_m > gm, 1.0, tie)
            sel = jnp.where(rank < min(MOBA_TOPK, nb), jnp.where(past, 1.0, 0.0), 0.0)
            for qb in range(nb):
                sel_ref[hh, qb] = sel[:, qb * blk:(qb + 1) * blk]

    def scores(qi, slot):
        past = qi * blk
        for hh in range(hp):
            cs = slice(hh * dh, (hh + 1) * dh)
            q = q_ref[past:past + blk, cs]
            s_diag = lax.dot_general(k_ref[past:past + blk, cs], q, NT_DIMS,
                                     preferred_element_type=F32) + bias_ref[hh, 0]
            s_diag = jnp.where(causal, s_diag, NEG_INF)
            s_ref[slot, hh, past:past + blk, :] = s_diag
            m = jnp.max(s_diag, axis=0, keepdims=True)
            if past:
                s_past = lax.dot_general(k_ref[0:past, cs], q, NT_DIMS,
                                         preferred_element_type=F32)
                for j in range(qi):
                    s_j = s_past[j * blk:(j + 1) * blk]
                    if j == qi - 1:
                        s_j = s_j + bias_ref[hh, 1]
                    sel = sel_ref[hh, qi, j:j + 1, :]
                    s_j = jnp.where(sel > 0.5, s_j, NEG_INF)
                    s_ref[slot, hh, j * blk:(j + 1) * blk, :] = s_j
                    m = jnp.maximum(m, jnp.max(s_j, axis=0, keepdims=True))
            m_ref[slot, hh] = m

    def attend(qi, slot):
        kv_len = (qi + 1) * blk
        for hh in range(hp):
            cs = slice(hh * dh, (hh + 1) * dh)
            p = jnp.exp2((s_ref[slot, hh, 0:kv_len, :] - m_ref[slot, hh]) * (scale * LOG2E))
            l = jnp.sum(p, axis=0, keepdims=True)
            p_ref[hh, 0:kv_len, :] = p.astype(BF16)
            acc = lax.dot_general(v_ref[0:kv_len, cs], p_ref[hh, 0:kv_len, :], TN_DIMS,
                                  preferred_element_type=F32)
            o_ref[:, cs] = (acc / l).T.astype(BF16)

    for r in range(nb):
        @pl.when(n % nb == r)
        def _(r=r):
            if r == 0:
                select_blocks()
            scores(r, r % 2)
            attend((r - 1) % nb, (r - 1) % 2)


def _moba_attn(qkv, bias, b, s, hp=4):
    assert MOBA_BLOCK + 1 >= REL_MAX_DIST and s % MOBA_BLOCK == 0
    t = b * s
    blk = MOBA_BLOCK
    nb = s // blk
    assert nb % 2 == 0
    dh = MOBA_HEAD_DIM
    groups = MOBA_HEADS // hp
    units = b * groups * nb

    def unit(n):
        return n // (groups * nb), (n // nb) % groups, n % nb

    def cur(n):
        return unit(jnp.minimum(n, units - 1))

    def prev(n):
        return unit(jnp.maximum(n - 1, 0))

    def q_map(n):
        bb, g, _ = cur(n)
        return bb, g

    def k_map(n):
        bb, g, _ = cur(n)
        return bb, groups + g

    def bias_map(n):
        return cur(n)[1], 0, 0, 0

    def v_map(n):
        bb, g, _ = prev(n)
        return bb, 2 * groups + g

    def o_map(n):
        bb, g, i = prev(n)
        return bb * nb + i, g

    return pl.pallas_call(
        functools.partial(_moba_attn_kernel, hp=hp, nb=nb, scale=dh ** -0.5),
        grid=(units + 1,),
        in_specs=[pl.BlockSpec((s, hp * dh), q_map),
                  pl.BlockSpec((s, hp * dh), k_map),
                  pl.BlockSpec((s, hp * dh), v_map),
                  pl.BlockSpec((hp, 2, blk, blk), bias_map)],
        out_specs=pl.BlockSpec((blk, hp * dh), o_map),
        out_shape=jax.ShapeDtypeStruct((t, MOBA_HEADS * dh), BF16),
        scratch_shapes=[pltpu.VMEM((hp, nb, nb, blk), F32),
                        pltpu.VMEM((2, hp, s, blk), F32),
                        pltpu.VMEM((2, hp, 1, blk), F32),
                        pltpu.VMEM((hp, s, blk), BF16)],
        compiler_params=_params(("arbitrary",)),
        name="moba_attn",
    )(qkv, qkv, qkv, bias)


SSD_GN = SSD_GROUPS * SSD_STATE
SSD_GW = SSD_INNER // SSD_GROUPS


def _softplus(x):
    return jnp.maximum(x, 0.0) + jnp.log1p(jnp.exp(-jnp.abs(x)))


def _silu(x):
    return x * jax.nn.sigmoid(x)


def _expand_heads(v, e3):
    h1 = v.astype(BF16)
    r1 = v - h1.astype(F32)
    h2 = r1.astype(BF16)
    h3 = (r1 - h2.astype(F32)).astype(BF16)
    return jnp.dot(jnp.concatenate([h1, h2, h3], axis=1), e3, preferred_element_type=F32)


def _ssd_proj_kernel(x_ref, g_ref, w_ref, cw_ref, cb_ref, z_ref, xs_ref, bc_ref, dt_ref,
                     xpad_ref, *, tiles_per_seq):
    i = pl.program_id(0)
    tm = x_ref.shape[0]
    halo = SUBLANES
    c_xbc = SSD_INNER
    c_dt = SSD_INNER + SSD_CONV_DIM

    @pl.when(i % tiles_per_seq == 0)
    def _():
        xpad_ref[0:halo, :] = jnp.zeros((halo, SSD_CONV_DIM), F32)

    h = _rms(x_ref[...], g_ref[...]).astype(BF16)
    dt_ref[...] = jnp.dot(h, w_ref[:, c_dt:c_dt + LANES], preferred_element_type=F32)
    for c0 in range(0, SSD_CONV_DIM, 512):
        cs = slice(c0, c0 + 512)
        if c0 < SSD_INNER:
            z_ref[:, cs] = jnp.dot(h, w_ref[:, cs], preferred_element_type=F32)
        raw = jnp.dot(h, w_ref[:, c_xbc + c0:c_xbc + c0 + 512], preferred_element_type=F32)
        xpad_ref[halo:halo + tm, cs] = raw
        conv = cb_ref[:, cs] + cw_ref[SSD_CONV - 1:SSD_CONV, cs] * raw
        for k in range(SSD_CONV - 1):
            r0 = halo - (SSD_CONV - 1) + k
            conv = conv + cw_ref[k:k + 1, cs] * xpad_ref[r0:r0 + tm, cs]
        xpad_ref[0:halo, cs] = xpad_ref[tm:tm + halo, cs]
        xc = _silu(conv)
        if c0 < SSD_INNER:
            xs_ref[:, cs] = xc
        else:
            bc_ref[:, c0 - SSD_INNER:c0 - SSD_INNER + 512] = xc.astype(BF16)


def _ssd_proj(x, g, w_in, conv_w, conv_b, s, tm=512):
    t, d = x.shape
    assert SSD_CONV_DIM % 512 == 0 and SSD_INNER % 1024 == 0 and s % tm == 0
    w_pad = jnp.pad(w_in, ((0, 0), (0, LANES - SSD_HEADS))).astype(BF16)
    n = w_pad.shape[1]
    row = lambda i: (i, 0)
    fixed = lambda i: (0, 0)
    resident = lambda shape: pl.BlockSpec(shape, fixed, pipeline_mode=pl.Buffered(1))
    return pl.pallas_call(
        functools.partial(_ssd_proj_kernel, tiles_per_seq=s // tm),
        grid=(t // tm,),
        in_specs=[pl.BlockSpec((tm, d), row),
                  resident((1, d)),
                  resident((d, n)),
                  resident((SSD_CONV, SSD_CONV_DIM)),
                  resident((1, SSD_CONV_DIM))],
        out_specs=[pl.BlockSpec((tm, SSD_INNER), row),
                   pl.BlockSpec((tm, SSD_INNER), row),
                   pl.BlockSpec((tm, 2 * SSD_GN), row),
                   pl.BlockSpec((tm, LANES), row)],
        out_shape=[jax.ShapeDtypeStruct((t, SSD_INNER), F32),
                   jax.ShapeDtypeStruct((t, SSD_INNER), F32),
                   jax.ShapeDtypeStruct((t, 2 * SSD_GN), BF16),
                   jax.ShapeDtypeStruct((t, LANES), F32)],
        scratch_shapes=[pltpu.VMEM((tm + SUBLANES, SSD_CONV_DIM), F32)],
        compiler_params=_params(("arbitrary",), 56),
        name="ssd_proj",
    )(x, g.reshape(1, d), w_pad, conv_w, conv_b.reshape(1, -1))


def _ssd_kernel(z_ref, xs_ref, bc_ref, dt_ref, dtb_ref, alog_ref, dexp_ref,
                nw_ref, e3_ref, o_ref, st_ref, *, chunks):
    @pl.when(pl.program_id(1) == 0)
    def _():
        st_ref[...] = jnp.zeros_like(st_ref)

    for cc in range(chunks):
        rows = slice(cc * SSD_CHUNK, (cc + 1) * SSD_CHUNK)
        _ssd_chunk(z_ref.at[rows], xs_ref.at[rows], bc_ref.at[rows], dt_ref.at[rows],
                   dtb_ref, alog_ref, dexp_ref, nw_ref, e3_ref, o_ref.at[rows], st_ref)


def _ssd_chunk(z_ref, xs_ref, bc_ref, dt_ref, dtb_ref, alog_ref, dexp_ref,
               nw_ref, e3_ref, o_ref, st_ref):
    L = SSD_CHUNK
    P = SSD_HEAD_DIM
    N = SSD_STATE
    xs = xs_ref[...]
    bm = bc_ref[:, :SSD_GN]
    cm = bc_ref[:, SSD_GN:]
    xs_b = xs.astype(BF16)

    dt = _softplus(dt_ref[...] + dtb_ref[...])
    a = -jnp.exp(alog_ref[...])
    row = lax.broadcasted_iota(jnp.int32, (L, L), 0)
    col = lax.broadcasted_iota(jnp.int32, (L, L), 1)
    tril = row >= col
    a_cs = jnp.dot(jnp.where(tril, 1.0, 0.0).astype(F32), dt * a,
                   precision=HIGHEST, preferred_element_type=F32)
    a_cs_t = a_cs.T
    dt_t = dt.T
    e3 = e3_ref[...]
    decay_exp = _expand_heads(jnp.exp(a_cs), e3)
    w_exp = _expand_heads(jnp.exp(a_cs[L - 1:L, :] - a_cs) * dt, e3)
    xw = (xs * w_exp).astype(BF16)
    lane = lax.broadcasted_iota(jnp.int32, (L, LANES), 1)

    ys = []
    for g in range(SSD_GROUPS):
        gs = slice(g * SSD_GW, (g + 1) * SSD_GW)
        b_g = bm[:, g * N:(g + 1) * N]
        c_g = cm[:, g * N:(g + 1) * N]
        cb = lax.dot_general(c_g, b_g, NT_DIMS, preferred_element_type=F32)
        state = st_ref[:, gs]
        y_off = jnp.dot(c_g, state.astype(BF16), preferred_element_type=F32) * decay_exp[:, gs]
        new_state = lax.dot_general(b_g, xw[:, gs], TN_DIMS, preferred_element_type=F32)
        st_ref[:, gs] = state * decay_exp[L - 1:L, gs] + new_state
        y_diag = []
        for pr in range(SSD_HEADS_PER_GROUP // 2):
            h0 = g * SSD_HEADS_PER_GROUP + 2 * pr
            mats = []
            for hr in (h0, h0 + 1):
                seg = a_cs[:, hr:hr + 1] - a_cs_t[hr:hr + 1, :]
                decay = jnp.exp(jnp.where(tril, seg, NEG_INF))
                mats.append((cb * decay * dt_t[hr:hr + 1, :]).astype(BF16))
            xp = xs_b[:, h0 * P:(h0 + 2) * P]
            zero = jnp.zeros_like(xp)
            x2 = jnp.concatenate([jnp.where(lane < P, xp, zero),
                                  jnp.where(lane >= P, xp, zero)], axis=0)
            y_diag.append(jnp.dot(jnp.concatenate(mats, axis=1), x2,
                                  preferred_element_type=F32))
        ys.append(jnp.concatenate(y_diag, axis=1) + y_off)
    y = jnp.concatenate(ys, axis=1) + dexp_ref[...] * xs
    gated = y * _silu(z_ref[...])
    outs = []
    for g in range(SSD_GROUPS):
        gg = gated[:, g * SSD_GW:(g + 1) * SSD_GW]
        ms = jnp.mean(gg * gg, axis=-1, keepdims=True)
        outs.append(gg * lax.rsqrt(ms + NORM_EPS))
    o_ref[...] = (jnp.concatenate(outs, axis=1) * nw_ref[...]).astype(BF16)


def _ssd_scan(z, xs, bc, dt, dt_bias, a_log, d_skip, norm_w, b, s, chunks=4):
    t = b * s
    L = chunks * SSD_CHUNK
    assert s % L == 0
    nc = s // L
    pad_h = lambda v: jnp.pad(v, (0, LANES - SSD_HEADS)).reshape(1, LANES)
    e = np.zeros((LANES, SSD_INNER), np.float32)
    for r in range(SSD_HEADS):
        e[r, r * SSD_HEAD_DIM:(r + 1) * SSD_HEAD_DIM] = 1.0
    e3 = jnp.asarray(np.concatenate([e, e, e], axis=0), dtype=BF16)
    d_exp = jnp.repeat(d_skip, SSD_HEAD_DIM).reshape(1, SSD_INNER)
    row = lambda bb, c: (bb * nc + c, 0)
    fixed = lambda bb, c: (0, 0)
    return pl.pallas_call(
        functools.partial(_ssd_kernel, chunks=chunks),
        grid=(b, nc),
        in_specs=[pl.BlockSpec((L, SSD_INNER), row),
                  pl.BlockSpec((L, SSD_INNER), row),
                  pl.BlockSpec((L, 2 * SSD_GN), row),
                  pl.BlockSpec((L, LANES), row),
                  pl.BlockSpec((1, LANES), fixed),
                  pl.BlockSpec((1, LANES), fixed),
                  pl.BlockSpec((1, SSD_INNER), fixed),
                  pl.BlockSpec((1, SSD_INNER), fixed),
                  pl.BlockSpec((3 * LANES, SSD_INNER), fixed)],
        out_specs=pl.BlockSpec((L, SSD_INNER), row),
        out_shape=jax.ShapeDtypeStruct((t, SSD_INNER), BF16),
        scratch_shapes=[pltpu.VMEM((SSD_STATE, SSD_INNER), F32)],
        compiler_params=_params(("parallel", "arbitrary")),
        name="ssd_scan",
    )(z, xs, bc, dt, pad_h(dt_bias), pad_h(a_log), d_exp, norm_w.reshape(1, -1), e3)


def _post_kernel(x_ref, a_ref, wo_ref, g_ref, w1_ref, w2_ref, gf_ref, o_ref,
                 *, final_norm, th):
    x1 = x_ref[...] + jnp.dot(a_ref[...], wo_ref[...], preferred_element_type=F32)
    h = _rms(x1, g_ref[...]).astype(BF16)
    y = x1
    for c0 in range(0, w1_ref.shape[1], th):
        u = jnp.maximum(
            jnp.dot(h, w1_ref[:, c0:c0 + th], preferred_element_type=F32), 0.0)
        y = y + jnp.dot((u * u).astype(BF16), w2_ref[c0:c0 + th, :],
                        preferred_element_type=F32)
    if final_norm:
        y = _rms(y, gf_ref[...])
    o_ref[...] = y


def _post(x, a, w_o, g, w1_all, w2_all, layer, g_final, final_norm, tm=512, th=1024):
    t, d = x.shape
    ka = a.shape[1]
    hid = w1_all.shape[2]
    row = lambda i: (i, 0)
    fixed = lambda i: (0, 0)
    resident = lambda shape: pl.BlockSpec(shape, fixed, pipeline_mode=pl.Buffered(1))
    of_layer = lambda shape: pl.BlockSpec((None,) + shape, lambda i: (layer, 0, 0),
                                          pipeline_mode=pl.Buffered(1))
    return pl.pallas_call(
        functools.partial(_post_kernel, final_norm=final_norm, th=th),
        grid=(t // tm,),
        in_specs=[pl.BlockSpec((tm, d), row),
                  pl.BlockSpec((tm, ka), row),
                  resident((ka, d)),
                  resident((1, d)),
                  of_layer((d, hid)),
                  of_layer((hid, d)),
                  resident((1, d))],
        out_specs=pl.BlockSpec((tm, d), row),
        out_shape=jax.ShapeDtypeStruct((t, d), F32),
        compiler_params=_params(("parallel",), 56),
        name="post_mlp",
    )(x, a, w_o.astype(BF16), g.reshape(1, d), w1_all, w2_all, g_final.reshape(1, d))


def _mla_mixer(x, g, rope, w_in, q_norm, w_uq, kv_norm, w_ukv, b, s):
    q, kv, kpe = _mla_proj(x, g, w_in, q_norm, w_uq, kv_norm, w_ukv, rope)
    return _mla_attn(q, kv, kpe, b, s)


def _moba_mixer(x, g, w_qkv, bias, b, s):
    n = w_qkv.shape[1]
    (qkv,) = _norm_proj(x, g, w_qkv.astype(BF16), (n,), (BF16,), tm=1024)
    return _moba_attn(qkv, bias, b, s)


def _ssd_mixer(x, g, w_in, conv_w, conv_b, dt_bias, a_log, d_skip, norm_w, b, s):
    z, xs, bc, dt = _ssd_proj(x, g, w_in, conv_w, conv_b, s)
    return _ssd_scan(z, xs, bc, dt, dt_bias, a_log, d_skip, norm_w, b, s)


def kernel(x, positions, norm_mix, norm_mlp, norm_final, rel_bias, mla_w_in, mla_q_norm, mla_w_uq, mla_kv_norm, mla_w_ukv, mla_w_o, moba_w_qkv, moba_w_o, ssd_w_in, ssd_conv_w, ssd_conv_b, ssd_dt_bias, ssd_a_log, ssd_d, ssd_norm, ssd_w_out, mlp_w1, mlp_w2):
    b, s, d = x.shape
    xf = x.reshape(b * s, d)
    rope = None
    bias = None
    w1_all = mlp_w1.astype(BF16)
    w2_all = mlp_w2.astype(BF16)
    for i in range(DEPTH):
        kind, j = i % N_MIXERS, i // N_MIXERS
        if kind == 0:
            if rope is None:
                rope = _rope_tables(positions)
            a = _mla_mixer(xf, norm_mix[i], rope, mla_w_in[j], mla_q_norm[j], mla_w_uq[j],
                           mla_kv_norm[j], mla_w_ukv[j], b, s)
            w_o = mla_w_o[j]
        elif kind == 1:
            if bias is None:
                bias = _moba_bias(rel_bias)
            a = _moba_mixer(xf, norm_mix[i], moba_w_qkv[j], bias, b, s)
            w_o = moba_w_o[j]
        else:
            a = _ssd_mixer(xf, norm_mix[i], ssd_w_in[j], ssd_conv_w[j], ssd_conv_b[j],
                           ssd_dt_bias[j], ssd_a_log[j], ssd_d[j], ssd_norm[j], b, s)
            w_o = ssd_w_out[j]
        xf = _post(xf, a, w_o, norm_mlp[i], w1_all, w2_all, i, norm_final,
                   final_norm=(i == DEPTH - 1))
    return xf.reshape(b, s, d)
```

```python
import functools
import math

import numpy as np
import jax
import jax.numpy as jnp
from jax import lax
from jax.experimental import pallas as pl
from jax.experimental.pallas import tpu as pltpu

F32 = jnp.float32
BF16 = jnp.bfloat16
HIGHEST = lax.Precision.HIGHEST

D_MODEL = 1024
DEPTH = 4
N_MIXERS = 3
NORM_EPS = 1e-6
NEG_INF = -1e30

MLA_HEADS = 16
MLA_Q_RANK = 256
MLA_KV_RANK = 256
MLA_NOPE = 128
MLA_ROPE = 64
MLA_V = 128
ROPE_BASE = 10000.0

MOBA_HEADS = 8
MOBA_HEAD_DIM = D_MODEL // MOBA_HEADS
MOBA_BLOCK = 256
MOBA_TOPK = 3
REL_BUCKETS = 32
REL_MAX_DIST = 128

SSD_INNER = 2 * D_MODEL
SSD_HEAD_DIM = 64
SSD_HEADS = SSD_INNER // SSD_HEAD_DIM
SSD_GROUPS = 2
SSD_HEADS_PER_GROUP = SSD_HEADS // SSD_GROUPS
SSD_STATE = 128
SSD_CONV = 4
SSD_CONV_DIM = SSD_INNER + 2 * SSD_GROUPS * SSD_STATE
SSD_CHUNK = 128

MLP_HIDDEN = 4 * D_MODEL

LANES = 128
SUBLANES = 8
MIB = 1024 * 1024

NT_DIMS = (((1,), (1,)), ((), ()))
TN_DIMS = (((0,), (0,)), ((), ()))

TM_NORM_PROJ = 1024
TM_MLA_PROJ = 512
TM_SSD_PROJ = 512
TM_POST = 512
TH_POST = 1024
MLA_ATTN_HEADS = 4
MLA_ATTN_TQ = 512
MOBA_ATTN_HEADS = 4
SSD_CHUNKS_PER_STEP = 4
VMEM_MIB = 48
VMEM_MIB_LARGE = 56


def _params(semantics, vmem_mib=VMEM_MIB):
    return pltpu.CompilerParams(dimension_semantics=semantics,
                                vmem_limit_bytes=vmem_mib * MIB)


def _rms(x, g):
    ms = jnp.mean(x * x, axis=-1, keepdims=True)
    return x * lax.rsqrt(ms + NORM_EPS) * g


def _norm_proj_kernel(x_ref, g_ref, w_ref, *o_refs, splits, col_chunk):
    h = _rms(x_ref[...], g_ref[...]).astype(BF16)
    off = 0
    for o_ref, n in zip(o_refs, splits):
        for c0 in range(0, n, col_chunk):
            c1 = min(c0 + col_chunk, n)
            o_ref[:, c0:c1] = jnp.dot(
                h, w_ref[:, off + c0:off + c1],
                preferred_element_type=F32).astype(o_ref.dtype)
        off += n


def _norm_proj(x, g, w, splits, dtypes, tm):
    t, d = x.shape
    n = w.shape[1]
    assert sum(splits) == n and t % tm == 0
    return pl.pallas_call(
        functools.partial(_norm_proj_kernel, splits=tuple(splits), col_chunk=1024),
        grid=(t // tm,),
        in_specs=[pl.BlockSpec((tm, d), lambda i: (i, 0)),
                  pl.BlockSpec((1, d), lambda i: (0, 0)),
                  pl.BlockSpec((d, n), lambda i: (0, 0))],
        out_specs=[pl.BlockSpec((tm, s), lambda i: (i, 0)) for s in splits],
        out_shape=[jax.ShapeDtypeStruct((t, s), dt) for s, dt in zip(splits, dtypes)],
        compiler_params=_params(("parallel",), VMEM_MIB_LARGE),
        name="norm_proj",
    )(x, g.reshape(1, d), w)


def _rope_table_kernel(pos_ref, freq_ref, cos_ref, sin_ref):
    ang = pos_ref[...].astype(F32) * freq_ref[...]
    cos_ref[...] = jnp.cos(ang)
    sin_ref[...] = jnp.sin(ang)


def _rope_tables(positions):
    b, s = positions.shape
    t = b * s
    half = MLA_ROPE // 2
    per_row = LANES // half
    rows = t // per_row
    inv_freq = ROPE_BASE ** (-(jnp.arange(0, MLA_ROPE, 2, dtype=F32) / MLA_ROPE))
    pos_rep = jnp.repeat(positions.reshape(rows, per_row), half, axis=1)
    freq = jnp.tile(inv_freq, per_row).reshape(1, LANES)
    tr = 512
    cos, sin = pl.pallas_call(
        _rope_table_kernel,
        grid=(rows // tr,),
        in_specs=[pl.BlockSpec((tr, LANES), lambda i: (i, 0)),
                  pl.BlockSpec((1, LANES), lambda i: (0, 0))],
        out_specs=[pl.BlockSpec((tr, LANES), lambda i: (i, 0))] * 2,
        out_shape=[jax.ShapeDtypeStruct((rows, LANES), F32)] * 2,
        compiler_params=_params(("parallel",)),
        name="rope_tables",
    )(pos_rep, freq)
    cos = cos.reshape(t, half)
    sin = sin.reshape(t, half)
    zeros = lambda n: jnp.zeros((t, n), F32)
    cos_t = jnp.concatenate([cos, cos, zeros(LANES - 2 * half)], axis=1)
    sin_a = jnp.concatenate([-sin, zeros(LANES - half)], axis=1)
    sin_b = jnp.concatenate([zeros(half), sin, zeros(LANES - 2 * half)], axis=1)
    return cos_t, sin_a, sin_b


def _rope128(v, cos_t, sin_a, sin_b):
    half = MLA_ROPE // 2
    return (v * cos_t + pltpu.roll(v, LANES - half, 1) * sin_a
            + pltpu.roll(v, half, 1) * sin_b)


MLA_QW = 2 * LANES
MLA_KVW = MLA_NOPE + MLA_V
MLA_IN_PAD = MLA_Q_RANK + MLA_KV_RANK + LANES


def _mla_proj_kernel(x_ref, g_ref, win_ref, qn_ref, kvn_ref, wuq_ref, wukv_ref,
                     cos_ref, sa_ref, sb_ref, q_ref, kv_ref, kpe_ref):
    h = _rms(x_ref[...], g_ref[...]).astype(BF16)
    proj = jnp.dot(h, win_ref[...], preferred_element_type=F32)
    cq = _rms(proj[:, :MLA_Q_RANK], qn_ref[...]).astype(BF16)
    ckv = _rms(proj[:, MLA_Q_RANK:MLA_Q_RANK + MLA_KV_RANK], kvn_ref[...]).astype(BF16)
    cos_t, sin_a, sin_b = cos_ref[...], sa_ref[...], sb_ref[...]
    kpe_ref[...] = _rope128(proj[:, MLA_Q_RANK + MLA_KV_RANK:], cos_t, sin_a, sin_b).astype(BF16)
    for hh in range(MLA_HEADS):
        c0 = hh * MLA_QW
        qh = jnp.dot(cq, wuq_ref[:, c0:c0 + MLA_QW], preferred_element_type=F32)
        q_ref[:, c0:c0 + MLA_NOPE] = qh[:, :MLA_NOPE].astype(BF16)
        q_ref[:, c0 + MLA_NOPE:c0 + MLA_QW] = _rope128(
            qh[:, MLA_NOPE:], cos_t, sin_a, sin_b).astype(BF16)
    n_kv = MLA_HEADS * MLA_KVW
    for c0 in range(0, n_kv, 1024):
        kv_ref[:, c0:c0 + 1024] = jnp.dot(
            ckv, wukv_ref[:, c0:c0 + 1024], preferred_element_type=F32).astype(BF16)


def _mla_proj(x, g, w_in, q_norm, w_uq, kv_norm, w_ukv, rope, tm=TM_MLA_PROJ):
    t, d = x.shape
    nq = MLA_HEADS * MLA_QW
    nkv = MLA_HEADS * MLA_KVW
    w_in_p = jnp.pad(w_in, ((0, 0), (0, MLA_IN_PAD - w_in.shape[1]))).astype(BF16)
    w_uq_p = jnp.pad(w_uq.reshape(MLA_Q_RANK, MLA_HEADS, MLA_NOPE + MLA_ROPE),
                     ((0, 0), (0, 0), (0, MLA_QW - MLA_NOPE - MLA_ROPE))
                     ).reshape(MLA_Q_RANK, nq).astype(BF16)
    w_ukv_p = jnp.swapaxes(w_ukv.reshape(MLA_KV_RANK, MLA_HEADS, 2, MLA_NOPE), 1, 2
                           ).reshape(MLA_KV_RANK, nkv).astype(BF16)
    row = lambda i: (i, 0)
    fixed = lambda i: (0, 0)
    return pl.pallas_call(
        _mla_proj_kernel,
        grid=(t // tm,),
        in_specs=[pl.BlockSpec((tm, d), row),
                  pl.BlockSpec((1, d), fixed),
                  pl.BlockSpec((d, MLA_IN_PAD), fixed),
                  pl.BlockSpec((1, MLA_Q_RANK), fixed),
                  pl.BlockSpec((1, MLA_KV_RANK), fixed),
                  pl.BlockSpec((MLA_Q_RANK, nq), fixed),
                  pl.BlockSpec((MLA_KV_RANK, nkv), fixed),
                  pl.BlockSpec((tm, LANES), row),
                  pl.BlockSpec((tm, LANES), row),
                  pl.BlockSpec((tm, LANES), row)],
        out_specs=[pl.BlockSpec((tm, nq), row),
                   pl.BlockSpec((tm, nkv), row),
                   pl.BlockSpec((tm, LANES), row)],
        out_shape=[jax.ShapeDtypeStruct((t, nq), BF16),
                   jax.ShapeDtypeStruct((t, nkv), BF16),
                   jax.ShapeDtypeStruct((t, LANES), BF16)],
        compiler_params=_params(("parallel",)),
        name="mla_proj",
    )(x, g.reshape(1, d), w_in_p, q_norm.reshape(1, -1), kv_norm.reshape(1, -1),
      w_uq_p, w_ukv_p, *rope)


LOG2E = math.log2(math.e)


def _mla_attn_kernel(q_ref, k_ref, kpe_ref, v_ref, o_ref, s_ref, m_ref,
                     *, hp, tq, nq, scale):
    n = pl.program_id(0)
    c = scale * LOG2E
    kidx = lax.broadcasted_iota(jnp.int32, (tq, tq), 0)
    qidx = lax.broadcasted_iota(jnp.int32, (tq, tq), 1)
    causal = kidx <= qidx

    @pl.when(n == 0)
    def _():
        s_ref[...] = jnp.zeros_like(s_ref)
        m_ref[...] = jnp.zeros_like(m_ref)

    def scores(qi, slot):
        past = qi * tq
        kpe = kpe_ref[0:past + tq, :]
        for hh in range(hp):
            kc = jnp.concatenate(
                [k_ref[0:past + tq, hh * MLA_NOPE:(hh + 1) * MLA_NOPE], kpe], axis=1)
            q = q_ref[:, hh * MLA_QW:(hh + 1) * MLA_QW]
            s_diag = jnp.where(
                causal, lax.dot_general(kc[past:], q, NT_DIMS, preferred_element_type=F32),
                NEG_INF)
            s_ref[slot, hh, past:past + tq, :] = s_diag
            m = jnp.max(s_diag, axis=0, keepdims=True)
            if past:
                s_past = lax.dot_general(kc[:past], q, NT_DIMS, preferred_element_type=F32)
                s_ref[slot, hh, 0:past, :] = s_past
                m = jnp.maximum(m, jnp.max(s_past, axis=0, keepdims=True))
            m_ref[slot, hh] = m

    def attend(qi, slot):
        kv_len = (qi + 1) * tq
        for hh in range(hp):
            p = jnp.exp2((s_ref[slot, hh, 0:kv_len, :] - m_ref[slot, hh]) * c)
            l = jnp.sum(p, axis=0, keepdims=True)
            acc = lax.dot_general(v_ref[0:kv_len, hh * MLA_V:(hh + 1) * MLA_V],
                                  p.astype(BF16), TN_DIMS, preferred_element_type=F32)
            o_ref[:, hh * MLA_V:(hh + 1) * MLA_V] = (acc / l).T.astype(BF16)

    for r in range(nq):
        @pl.when(n % nq == r)
        def _(r=r):
            scores(r, r % 2)
            attend((r - 1) % nq, (r - 1) % 2)


def _mla_attn(q, kv, kpe, b, s, hp=MLA_ATTN_HEADS, tq=MLA_ATTN_TQ):
    assert MLA_NOPE == MLA_V
    t = b * s
    nq = s // tq
    assert nq % 2 == 0
    groups = MLA_HEADS // hp
    units = b * groups * nq
    scale = (MLA_NOPE + MLA_ROPE) ** -0.5

    def unit(n):
        return n // (groups * nq), (n // nq) % groups, n % nq

    def cur(n):
        return unit(jnp.minimum(n, units - 1))

    def last(n):
        return unit(jnp.maximum(n - 1, 0))

    def q_map(n):
        bb, g, i = cur(n)
        return bb * nq + i, g

    def k_map(n):
        bb, g, _ = cur(n)
        return bb, g

    def kpe_map(n):
        return cur(n)[0], 0

    def v_map(n):
        bb, g, _ = last(n)
        return bb, groups + g

    def o_map(n):
        bb, g, i = last(n)
        return bb * nq + i, g

    return pl.pallas_call(
        functools.partial(_mla_attn_kernel, hp=hp, tq=tq, nq=nq, scale=scale),
        grid=(units + 1,),
        in_specs=[pl.BlockSpec((tq, hp * MLA_QW), q_map),
                  pl.BlockSpec((s, hp * MLA_NOPE), k_map),
                  pl.BlockSpec((s, LANES), kpe_map),
                  pl.BlockSpec((s, hp * MLA_V), v_map)],
        out_specs=pl.BlockSpec((tq, hp * MLA_V), o_map),
        out_shape=jax.ShapeDtypeStruct((t, MLA_HEADS * MLA_V), BF16),
        scratch_shapes=[pltpu.VMEM((2, hp, s, tq), F32),
                        pltpu.VMEM((2, hp, 1, tq), F32)],
        compiler_params=_params(("arbitrary",), VMEM_MIB_LARGE),
        name="mla_attn",
    )(q, kv, kpe, kv)


def _t5_bucket_np(n):
    max_exact = REL_BUCKETS // 2
    nf = np.maximum(n, max_exact).astype(np.float32)
    large = max_exact + (np.log(nf / np.float32(max_exact))
                         / np.float32(math.log(REL_MAX_DIST / max_exact))
                         * np.float32(REL_BUCKETS - max_exact)).astype(np.int32)
    large = np.minimum(large, REL_BUCKETS - 1)
    return np.where(n < max_exact, n, large).astype(np.int32)


def _moba_bias_kernel(tab_ref, bucket_ref, o_ref, *, inv_scale):
    h = pl.program_id(0)
    bucket = bucket_ref[...]
    far = tab_ref[REL_BUCKETS - 1, h]
    acc = jnp.zeros(bucket.shape, F32)
    for k in range(REL_BUCKETS - 1):
        acc = jnp.where(bucket == k, tab_ref[k, h] - far, acc)
    o_ref[0] = acc * inv_scale


def _moba_bias(rel_bias):
    blk = MOBA_BLOCK
    key = np.arange(blk)[:, None]
    qry = np.arange(blk)[None, :]
    dist = np.stack([np.maximum(qry - key, 0), blk + qry - key])
    bucket = jnp.asarray(_t5_bucket_np(dist))
    return pl.pallas_call(
        functools.partial(_moba_bias_kernel, inv_scale=MOBA_HEAD_DIM ** 0.5),
        grid=(MOBA_HEADS,),
        in_specs=[pl.BlockSpec(memory_space=pltpu.SMEM),
                  pl.BlockSpec((2, blk, blk), lambda h: (0, 0, 0))],
        out_specs=pl.BlockSpec((1, 2, blk, blk), lambda h: (h, 0, 0, 0)),
        out_shape=jax.ShapeDtypeStruct((MOBA_HEADS, 2, blk, blk), F32),
        compiler_params=_params(("arbitrary",)),
        name="moba_bias",
    )(rel_bias, bucket)


def _moba_attn_kernel(q_ref, k_ref, v_ref, bias_ref, o_ref,
                      sel_ref, s_ref, m_ref, p_ref, *, hp, nb, scale):
    n = pl.program_id(0)
    blk = MOBA_BLOCK
    dh = MOBA_HEAD_DIM
    s_len = nb * blk
    kidx = lax.broadcasted_iota(jnp.int32, (blk, blk), 0)
    qidx = lax.broadcasted_iota(jnp.int32, (blk, blk), 1)
    causal = kidx <= qidx

    @pl.when(n == 0)
    def _():
        s_ref[...] = jnp.zeros_like(s_ref)
        m_ref[...] = jnp.zeros_like(m_ref)

    def select_blocks():
        nidx = lax.broadcasted_iota(jnp.int32, (nb, s_len), 0)
        qblk = lax.broadcasted_iota(jnp.int32, (nb, s_len), 1) // blk
        past = nidx < qblk
        for hh in range(hp):
            cs = slice(hh * dh, (hh + 1) * dh)
            k_mean = jnp.concatenate(
                [jnp.sum(k_ref[n * blk:(n + 1) * blk, cs].astype(F32), axis=0, keepdims=True)
                 for n in range(nb)], axis=0) / blk
            h1 = k_mean.astype(BF16)
            r1 = k_mean - h1.astype(F32)
            h2 = r1.astype(BF16)
            h3 = (r1 - h2.astype(F32)).astype(BF16)
            g3 = lax.dot_general(jnp.concatenate([h1, h2, h3], axis=0), q_ref[:, cs], NT_DIMS,
                                 preferred_element_type=F32)
            gate = g3[0:nb] + g3[nb:2 * nb] + g3[2 * nb:3 * nb]
            gm = jnp.where(past, gate, NEG_INF)
            rank = jnp.zeros((nb, s_len), F32)
            for m in range(nb):
                gm_m = gm[m:m + 1, :]
                tie = jnp.where(gm_m == gm, jnp.where(nidx > m, 1.0, 0.0), 0.0)
                rank = rank + jnp.where(gm_m > gm, 1.0, tie)
            sel = jnp.where(rank < min(MOBA_TOPK, nb), jnp.where(past, 1.0, 0.0), 0.0)
            for qb in range(nb):
                sel_ref[hh, qb] = sel[:, qb * blk:(qb + 1) * blk]

    def scores(qi, slot):
        past = qi * blk
        for hh in range(hp):
            cs = slice(hh * dh, (hh + 1) * dh)
            q = q_ref[past:past + blk, cs]
            s_diag = lax.dot_general(k_ref[past:past + blk, cs], q, NT_DIMS,
                                     preferred_element_type=F32) + bias_ref[hh, 0]
            s_diag = jnp.where(causal, s_diag, NEG_INF)
            s_ref[slot, hh, past:past + blk, :] = s_diag
            m = jnp.max(s_diag, axis=0, keepdims=True)
            if past:
                s_past = lax.dot_general(k_ref[0:past, cs], q, NT_DIMS,
                                         preferred_element_type=F32)
                for j in range(qi):
                    s_j = s_past[j * blk:(j + 1) * blk]
                    if j == qi - 1:
                        s_j = s_j + bias_ref[hh, 1]
                    sel = sel_ref[hh, qi, j:j + 1, :]
                    s_j = jnp.where(sel > 0.5, s_j, NEG_INF)
                    s_ref[slot, hh, j * blk:(j + 1) * blk, :] = s_j
                    m = jnp.maximum(m, jnp.max(s_j, axis=0, keepdims=True))
            m_ref[slot, hh] = m

    def attend(qi, slot):
        kv_len = (qi + 1) * blk
        for hh in range(hp):
            cs = slice(hh * dh, (hh + 1) * dh)
            p = jnp.exp2((s_ref[slot, hh, 0:kv_len, :] - m_ref[slot, hh]) * (scale * LOG2E))
            l = jnp.sum(p, axis=0, keepdims=True)
            p_ref[hh, 0:kv_len, :] = p.astype(BF16)
            acc = lax.dot_general(v_ref[0:kv_len, cs], p_ref[hh, 0:kv_len, :], TN_DIMS,
                                  preferred_element_type=F32)
            o_ref[:, cs] = (acc / l).T.astype(BF16)

    for r in range(nb):
        @pl.when(n % nb == r)
        def _(r=r):
            if r == 0:
                select_blocks()
            scores(r, r % 2)
            attend((r - 1) % nb, (r - 1) % 2)


def _moba_attn(qkv, bias, b, s, hp=MOBA_ATTN_HEADS):
    assert MOBA_BLOCK + 1 >= REL_MAX_DIST and s % MOBA_BLOCK == 0
    t = b * s
    blk = MOBA_BLOCK
    nb = s // blk
    assert nb % 2 == 0
    dh = MOBA_HEAD_DIM
    groups = MOBA_HEADS // hp
    units = b * groups * nb

    def unit(n):
        return n // (groups * nb), (n // nb) % groups, n % nb

    def cur(n):
        return unit(jnp.minimum(n, units - 1))

    def prev(n):
        return unit(jnp.maximum(n - 1, 0))

    def q_map(n):
        bb, g, _ = cur(n)
        return bb, g

    def k_map(n):
        bb, g, _ = cur(n)
        return bb, groups + g

    def bias_map(n):
        return cur(n)[1], 0, 0, 0

    def v_map(n):
        bb, g, _ = prev(n)
        return bb, 2 * groups + g

    def o_map(n):
        bb, g, i = prev(n)
        return bb * nb + i, g

    return pl.pallas_call(
        functools.partial(_moba_attn_kernel, hp=hp, nb=nb, scale=dh ** -0.5),
        grid=(units + 1,),
        in_specs=[pl.BlockSpec((s, hp * dh), q_map),
                  pl.BlockSpec((s, hp * dh), k_map),
                  pl.BlockSpec((s, hp * dh), v_map),
                  pl.BlockSpec((hp, 2, blk, blk), bias_map)],
        out_specs=pl.BlockSpec((blk, hp * dh), o_map),
        out_shape=jax.ShapeDtypeStruct((t, MOBA_HEADS * dh), BF16),
        scratch_shapes=[pltpu.VMEM((hp, nb, nb, blk), F32),
                        pltpu.VMEM((2, hp, s, blk), F32),
                        pltpu.VMEM((2, hp, 1, blk), F32),
                        pltpu.VMEM((hp, s, blk), BF16)],
        compiler_params=_params(("arbitrary",)),
        name="moba_attn",
    )(qkv, qkv, qkv, bias)


SSD_GN = SSD_GROUPS * SSD_STATE
SSD_GW = SSD_INNER // SSD_GROUPS


def _softplus(x):
    return jnp.maximum(x, 0.0) + jnp.log1p(jnp.exp(-jnp.abs(x)))


def _silu(x):
    return x * jax.nn.sigmoid(x)


def _expand_heads(v, e3):
    h1 = v.astype(BF16)
    r1 = v - h1.astype(F32)
    h2 = r1.astype(BF16)
    h3 = (r1 - h2.astype(F32)).astype(BF16)
    return jnp.dot(jnp.concatenate([h1, h2, h3], axis=1), e3, preferred_element_type=F32)


def _ssd_proj_kernel(x_ref, g_ref, w_ref, cw_ref, cb_ref, z_ref, xs_ref, bc_ref, dt_ref,
                     xpad_ref, *, tiles_per_seq):
    i = pl.program_id(0)
    tm = x_ref.shape[0]
    halo = SUBLANES
    c_xbc = SSD_INNER
    c_dt = SSD_INNER + SSD_CONV_DIM

    @pl.when(i % tiles_per_seq == 0)
    def _():
        xpad_ref[0:halo, :] = jnp.zeros((halo, SSD_CONV_DIM), F32)

    h = _rms(x_ref[...], g_ref[...]).astype(BF16)
    dt_ref[...] = jnp.dot(h, w_ref[:, c_dt:c_dt + LANES], preferred_element_type=F32)
    for c0 in range(0, SSD_CONV_DIM, 512):
        cs = slice(c0, c0 + 512)
        if c0 < SSD_INNER:
            z_ref[:, cs] = jnp.dot(h, w_ref[:, cs], preferred_element_type=F32)
        raw = jnp.dot(h, w_ref[:, c_xbc + c0:c_xbc + c0 + 512], preferred_element_type=F32)
        xpad_ref[halo:halo + tm, cs] = raw
        conv = cb_ref[:, cs] + cw_ref[SSD_CONV - 1:SSD_CONV, cs] * raw
        for k in range(SSD_CONV - 1):
            r0 = halo - (SSD_CONV - 1) + k
            conv = conv + cw_ref[k:k + 1, cs] * xpad_ref[r0:r0 + tm, cs]
        xpad_ref[0:halo, cs] = xpad_ref[tm:tm + halo, cs]
        xc = _silu(conv)
        if c0 < SSD_INNER:
            xs_ref[:, cs] = xc
        else:
            bc_ref[:, c0 - SSD_INNER:c0 - SSD_INNER + 512] = xc.astype(BF16)


def _ssd_proj(x, g, w_in, conv_w, conv_b, s, tm=TM_SSD_PROJ):
    t, d = x.shape
    assert SSD_CONV_DIM % 512 == 0 and SSD_INNER % 1024 == 0 and s % tm == 0
    w_pad = jnp.pad(w_in, ((0, 0), (0, LANES - SSD_HEADS))).astype(BF16)
    n = w_pad.shape[1]
    row = lambda i: (i, 0)
    fixed = lambda i: (0, 0)
    resident = lambda shape: pl.BlockSpec(shape, fixed, pipeline_mode=pl.Buffered(1))
    return pl.pallas_call(
        functools.partial(_ssd_proj_kernel, tiles_per_seq=s // tm),
        grid=(t // tm,),
        in_specs=[pl.BlockSpec((tm, d), row),
                  resident((1, d)),
                  resident((d, n)),
                  resident((SSD_CONV, SSD_CONV_DIM)),
                  resident((1, SSD_CONV_DIM))],
        out_specs=[pl.BlockSpec((tm, SSD_INNER), row),
                   pl.BlockSpec((tm, SSD_INNER), row),
                   pl.BlockSpec((tm, 2 * SSD_GN), row),
                   pl.BlockSpec((tm, LANES), row)],
        out_shape=[jax.ShapeDtypeStruct((t, SSD_INNER), F32),
                   jax.ShapeDtypeStruct((t, SSD_INNER), F32),
                   jax.ShapeDtypeStruct((t, 2 * SSD_GN), BF16),
                   jax.ShapeDtypeStruct((t, LANES), F32)],
        scratch_shapes=[pltpu.VMEM((tm + SUBLANES, SSD_CONV_DIM), F32)],
        compiler_params=_params(("arbitrary",), VMEM_MIB_LARGE),
        name="ssd_proj",
    )(x, g.reshape(1, d), w_pad, conv_w, conv_b.reshape(1, -1))


def _ssd_kernel(z_ref, xs_ref, bc_ref, dt_ref, dtb_ref, alog_ref, dexp_ref,
                nw_ref, e3_ref, o_ref, st_ref, *, chunks):
    @pl.when(pl.program_id(1) == 0)
    def _():
        st_ref[...] = jnp.zeros_like(st_ref)

    for cc in range(chunks):
        rows = slice(cc * SSD_CHUNK, (cc + 1) * SSD_CHUNK)
        _ssd_chunk(z_ref.at[rows], xs_ref.at[rows], bc_ref.at[rows], dt_ref.at[rows],
                   dtb_ref, alog_ref, dexp_ref, nw_ref, e3_ref, o_ref.at[rows], st_ref)


def _ssd_chunk(z_ref, xs_ref, bc_ref, dt_ref, dtb_ref, alog_ref, dexp_ref,
               nw_ref, e3_ref, o_ref, st_ref):
    L = SSD_CHUNK
    P = SSD_HEAD_DIM
    N = SSD_STATE
    xs = xs_ref[...]
    bm = bc_ref[:, :SSD_GN]
    cm = bc_ref[:, SSD_GN:]
    xs_b = xs.astype(BF16)

    dt = _softplus(dt_ref[...] + dtb_ref[...])
    a = -jnp.exp(alog_ref[...])
    row = lax.broadcasted_iota(jnp.int32, (L, L), 0)
    col = lax.broadcasted_iota(jnp.int32, (L, L), 1)
    tril = row >= col
    a_cs = jnp.dot(jnp.where(tril, 1.0, 0.0).astype(F32), dt * a,
                   precision=HIGHEST, preferred_element_type=F32)
    a_cs_t = a_cs.T
    dt_t = dt.T
    e3 = e3_ref[...]
    decay_exp = _expand_heads(jnp.exp(a_cs), e3)
    w_exp = _expand_heads(jnp.exp(a_cs[L - 1:L, :] - a_cs) * dt, e3)
    xw = (xs * w_exp).astype(BF16)
    lane = lax.broadcasted_iota(jnp.int32, (L, LANES), 1)

    ys = []
    for g in range(SSD_GROUPS):
        gs = slice(g * SSD_GW, (g + 1) * SSD_GW)
        b_g = bm[:, g * N:(g + 1) * N]
        c_g = cm[:, g * N:(g + 1) * N]
        cb = lax.dot_general(c_g, b_g, NT_DIMS, preferred_element_type=F32)
        state = st_ref[:, gs]
        y_off = jnp.dot(c_g, state.astype(BF16), preferred_element_type=F32) * decay_exp[:, gs]
        new_state = lax.dot_general(b_g, xw[:, gs], TN_DIMS, preferred_element_type=F32)
        st_ref[:, gs] = state * decay_exp[L - 1:L, gs] + new_state
        y_diag = []
        for pr in range(SSD_HEADS_PER_GROUP // 2):
            h0 = g * SSD_HEADS_PER_GROUP + 2 * pr
            mats = []
            for hr in (h0, h0 + 1):
                seg = a_cs[:, hr:hr + 1] - a_cs_t[hr:hr + 1, :]
                decay = jnp.exp(jnp.where(tril, seg, NEG_INF))
                mats.append((cb * decay * dt_t[hr:hr + 1, :]).astype(BF16))
            xp = xs_b[:, h0 * P:(h0 + 2) * P]
            zero = jnp.zeros_like(xp)
            x2 = jnp.concatenate([jnp.where(lane < P, xp, zero),
                                  jnp.where(lane >= P, xp, zero)], axis=0)
            y_diag.append(jnp.dot(jnp.concatenate(mats, axis=1), x2,
                                  preferred_element_type=F32))
        ys.append(jnp.concatenate(y_diag, axis=1) + y_off)
    y = jnp.concatenate(ys, axis=1) + dexp_ref[...] * xs
    gated = y * _silu(z_ref[...])
    outs = []
    for g in range(SSD_GROUPS):
        gg = gated[:, g * SSD_GW:(g + 1) * SSD_GW]
        ms = jnp.mean(gg * gg, axis=-1, keepdims=True)
        outs.append(gg * lax.rsqrt(ms + NORM_EPS))
    o_ref[...] = (jnp.concatenate(outs, axis=1) * nw_ref[...]).astype(BF16)


def _ssd_scan(z, xs, bc, dt, dt_bias, a_log, d_skip, norm_w, b, s,
              chunks=SSD_CHUNKS_PER_STEP):
    t = b * s
    L = chunks * SSD_CHUNK
    assert s % L == 0
    nc = s // L
    pad_h = lambda v: jnp.pad(v, (0, LANES - SSD_HEADS)).reshape(1, LANES)
    e = np.zeros((LANES, SSD_INNER), np.float32)
    for r in range(SSD_HEADS):
        e[r, r * SSD_HEAD_DIM:(r + 1) * SSD_HEAD_DIM] = 1.0
    e3 = jnp.asarray(np.concatenate([e, e, e], axis=0), dtype=BF16)
    d_exp = jnp.repeat(d_skip, SSD_HEAD_DIM).reshape(1, SSD_INNER)
    row = lambda bb, c: (bb * nc + c, 0)
    fixed = lambda bb, c: (0, 0)
    return pl.pallas_call(
        functools.partial(_ssd_kernel, chunks=chunks),
        grid=(b, nc),
        in_specs=[pl.BlockSpec((L, SSD_INNER), row),
                  pl.BlockSpec((L, SSD_INNER), row),
                  pl.BlockSpec((L, 2 * SSD_GN), row),
                  pl.BlockSpec((L, LANES), row),
                  pl.BlockSpec((1, LANES), fixed),
                  pl.BlockSpec((1, LANES), fixed),
                  pl.BlockSpec((1, SSD_INNER), fixed),
                  pl.BlockSpec((1, SSD_INNER), fixed),
                  pl.BlockSpec((3 * LANES, SSD_INNER), fixed)],
        out_specs=pl.BlockSpec((L, SSD_INNER), row),
        out_shape=jax.ShapeDtypeStruct((t, SSD_INNER), BF16),
        scratch_shapes=[pltpu.VMEM((SSD_STATE, SSD_INNER), F32)],
        compiler_params=_params(("parallel", "arbitrary")),
        name="ssd_scan",
    )(z, xs, bc, dt, pad_h(dt_bias), pad_h(a_log), d_exp, norm_w.reshape(1, -1), e3)


def _post_kernel(x_ref, a_ref, wo_ref, g_ref, w1_ref, w2_ref, gf_ref, o_ref,
                 *, final_norm, th):
    x1 = x_ref[...] + jnp.dot(a_ref[...], wo_ref[...], preferred_element_type=F32)
    h = _rms(x1, g_ref[...]).astype(BF16)
    y = x1
    for c0 in range(0, w1_ref.shape[1], th):
        u = jnp.maximum(
            jnp.dot(h, w1_ref[:, c0:c0 + th], preferred_element_type=F32), 0.0)
        y = y + jnp.dot((u * u).astype(BF16), w2_ref[c0:c0 + th, :],
                        preferred_element_type=F32)
    if final_norm:
        y = _rms(y, gf_ref[...])
    o_ref[...] = y


def _post(x, a, w_o, g, w1_all, w2_all, layer, g_final, final_norm,
          tm=TM_POST, th=TH_POST):
    t, d = x.shape
    ka = a.shape[1]
    hid = w1_all.shape[2]
    row = lambda i: (i, 0)
    fixed = lambda i: (0, 0)
    resident = lambda shape: pl.BlockSpec(shape, fixed, pipeline_mode=pl.Buffered(1))
    of_layer = lambda shape: pl.BlockSpec((None,) + shape, lambda i: (layer, 0, 0),
                                          pipeline_mode=pl.Buffered(1))
    return pl.pallas_call(
        functools.partial(_post_kernel, final_norm=final_norm, th=th),
        grid=(t // tm,),
        in_specs=[pl.BlockSpec((tm, d), row),
                  pl.BlockSpec((tm, ka), row),
                  resident((ka, d)),
                  resident((1, d)),
                  of_layer((d, hid)),
                  of_layer((hid, d)),
                  resident((1, d))],
        out_specs=pl.BlockSpec((tm, d), row),
        out_shape=jax.ShapeDtypeStruct((t, d), F32),
        compiler_params=_params(("parallel",), VMEM_MIB_LARGE),
        name="post_mlp",
    )(x, a, w_o.astype(BF16), g.reshape(1, d), w1_all, w2_all, g_final.reshape(1, d))


def _mla_mixer(x, g, rope, w_in, q_norm, w_uq, kv_norm, w_ukv, b, s):
    q, kv, kpe = _mla_proj(x, g, w_in, q_norm, w_uq, kv_norm, w_ukv, rope)
    return _mla_attn(q, kv, kpe, b, s)


def _moba_mixer(x, g, w_qkv, bias, b, s):
    n = w_qkv.shape[1]
    (qkv,) = _norm_proj(x, g, w_qkv.astype(BF16), (n,), (BF16,), tm=TM_NORM_PROJ)
    return _moba_attn(qkv, bias, b, s)


def _ssd_mixer(x, g, w_in, conv_w, conv_b, dt_bias, a_log, d_skip, norm_w, b, s):
    z, xs, bc, dt = _ssd_proj(x, g, w_in, conv_w, conv_b, s)
    return _ssd_scan(z, xs, bc, dt, dt_bias, a_log, d_skip, norm_w, b, s)


def kernel(x, positions, norm_mix, norm_mlp, norm_final, rel_bias, mla_w_in, mla_q_norm, mla_w_uq, mla_kv_norm, mla_w_ukv, mla_w_o, moba_w_qkv, moba_w_o, ssd_w_in, ssd_conv_w, ssd_conv_b, ssd_dt_bias, ssd_a_log, ssd_d, ssd_norm, ssd_w_out, mlp_w1, mlp_w2):
    b, s, d = x.shape
    xf = x.reshape(b * s, d)
    rope = None
    bias = None
    w1_all = mlp_w1.astype(BF16)
    w2_all = mlp_w2.astype(BF16)
    for i in range(DEPTH):
        kind, j = i % N_MIXERS, i // N_MIXERS
        if kind == 0:
            if rope is None:
                rope = _rope_tables(positions)
            a = _mla_mixer(xf, norm_mix[i], rope, mla_w_in[j], mla_q_norm[j], mla_w_uq[j],
                           mla_kv_norm[j], mla_w_ukv[j], b, s)
            w_o = mla_w_o[j]
        elif kind == 1:
            if bias is None:
                bias = _moba_bias(rel_bias)
            a = _moba_mixer(xf, norm_mix[i], moba_w_qkv[j], bias, b, s)
            w_o = moba_w_o[j]
        else:
            a = _ssd_mixer(xf, norm_mix[i], ssd_w_in[j], ssd_conv_w[j], ssd_conv_b[j],
                           ssd_dt_bias[j], ssd_a_log[j], ssd_d[j], ssd_norm[j], b, s)
            w_o = ssd_w_out[j]
        xf = _post(xf, a, w_o, norm_mlp[i], w1_all, w2_all, i, norm_final,
                   final_norm=(i == DEPTH - 1))
    return xf.reshape(b, s, d)
```

```python
import functools
import math

import numpy as np
import jax
import jax.numpy as jnp
from jax import lax
from jax.experimental import pallas as pl
from jax.experimental.pallas import tpu as pltpu

F32 = jnp.float32
BF16 = jnp.bfloat16
HIGHEST = lax.Precision.HIGHEST

D_MODEL = 1024
DEPTH = 4
N_MIXERS = 3
NORM_EPS = 1e-6
NEG_INF = -1e30

MLA_HEADS = 16
MLA_Q_RANK = 256
MLA_KV_RANK = 256
MLA_NOPE = 128
MLA_ROPE = 64
MLA_V = 128
ROPE_BASE = 10000.0

MOBA_HEADS = 8
MOBA_HEAD_DIM = D_MODEL // MOBA_HEADS
MOBA_BLOCK = 256
MOBA_TOPK = 3
REL_BUCKETS = 32
REL_MAX_DIST = 128

SSD_INNER = 2 * D_MODEL
SSD_HEAD_DIM = 64
SSD_HEADS = SSD_INNER // SSD_HEAD_DIM
SSD_GROUPS = 2
SSD_HEADS_PER_GROUP = SSD_HEADS // SSD_GROUPS
SSD_STATE = 128
SSD_CONV = 4
SSD_CONV_DIM = SSD_INNER + 2 * SSD_GROUPS * SSD_STATE
SSD_CHUNK = 128

MLP_HIDDEN = 4 * D_MODEL

LANES = 128
SUBLANES = 8
MIB = 1024 * 1024

NT_DIMS = (((1,), (1,)), ((), ()))
TN_DIMS = (((0,), (0,)), ((), ()))

TM_NORM_PROJ = 1024
TM_MLA_PROJ = 512
TM_SSD_PROJ = 512
TM_POST = 512
TH_POST = 1024
MLA_ATTN_HEADS = 4
MLA_ATTN_TQ = 512
MOBA_ATTN_HEADS = 4
SSD_CHUNKS_PER_STEP = 4
VMEM_MIB = 48
VMEM_MIB_LARGE = 56


def _params(semantics, vmem_mib=VMEM_MIB):
    return pltpu.CompilerParams(dimension_semantics=semantics,
                                vmem_limit_bytes=vmem_mib * MIB)


def _rms(x, g):
    ms = jnp.mean(x * x, axis=-1, keepdims=True)
    return x * lax.rsqrt(ms + NORM_EPS) * g


def _norm_proj_kernel(x_ref, g_ref, w_ref, *o_refs, splits, scales, col_chunk):
    h = _rms(x_ref[...], g_ref[...]).astype(BF16)
    off = 0
    for o_ref, n, scale in zip(o_refs, splits, scales):
        for c0 in range(0, n, col_chunk):
            c1 = min(c0 + col_chunk, n)
            r = jnp.dot(h, w_ref[:, off + c0:off + c1], preferred_element_type=F32)
            if scale is not None:
                r = r * scale
            o_ref[:, c0:c1] = r.astype(o_ref.dtype)
        off += n


def _norm_proj(x, g, w, splits, dtypes, scales, tm):
    t, d = x.shape
    n = w.shape[1]
    assert sum(splits) == n and t % tm == 0
    return pl.pallas_call(
        functools.partial(_norm_proj_kernel, splits=tuple(splits), scales=tuple(scales),
                          col_chunk=1024),
        grid=(t // tm,),
        in_specs=[pl.BlockSpec((tm, d), lambda i: (i, 0)),
                  pl.BlockSpec((1, d), lambda i: (0, 0)),
                  pl.BlockSpec((d, n), lambda i: (0, 0))],
        out_specs=[pl.BlockSpec((tm, s), lambda i: (i, 0)) for s in splits],
        out_shape=[jax.ShapeDtypeStruct((t, s), dt) for s, dt in zip(splits, dtypes)],
        compiler_params=_params(("parallel",), VMEM_MIB_LARGE),
        name="norm_proj",
    )(x, g.reshape(1, d), w)


def _rope_table_kernel(pos_ref, freq_ref, cos_ref, sin_ref):
    ang = pos_ref[...].astype(F32) * freq_ref[...]
    cos_ref[...] = jnp.cos(ang)
    sin_ref[...] = jnp.sin(ang)


def _rope_tables(positions):
    b, s = positions.shape
    t = b * s
    half = MLA_ROPE // 2
    per_row = LANES // half
    rows = t // per_row
    inv_freq = ROPE_BASE ** (-(jnp.arange(0, MLA_ROPE, 2, dtype=F32) / MLA_ROPE))
    pos_rep = jnp.repeat(positions.reshape(rows, per_row), half, axis=1)
    freq = jnp.tile(inv_freq, per_row).reshape(1, LANES)
    tr = 512
    cos, sin = pl.pallas_call(
        _rope_table_kernel,
        grid=(rows // tr,),
        in_specs=[pl.BlockSpec((tr, LANES), lambda i: (i, 0)),
                  pl.BlockSpec((1, LANES), lambda i: (0, 0))],
        out_specs=[pl.BlockSpec((tr, LANES), lambda i: (i, 0))] * 2,
        out_shape=[jax.ShapeDtypeStruct((rows, LANES), F32)] * 2,
        compiler_params=_params(("parallel",)),
        name="rope_tables",
    )(pos_rep, freq)
    cos = cos.reshape(t, half)
    sin = sin.reshape(t, half)
    zeros = lambda n: jnp.zeros((t, n), F32)
    cos_t = jnp.concatenate([cos, cos, zeros(LANES - 2 * half)], axis=1)
    sin_a = jnp.concatenate([-sin, zeros(LANES - half)], axis=1)
    sin_b = jnp.concatenate([zeros(half), sin, zeros(LANES - 2 * half)], axis=1)
    return cos_t, sin_a, sin_b


def _rope128(v, cos_t, sin_a, sin_b):
    half = MLA_ROPE // 2
    return (v * cos_t + pltpu.roll(v, LANES - half, 1) * sin_a
            + pltpu.roll(v, half, 1) * sin_b)


LOG2E = math.log2(math.e)
MLA_Q_SCALE = (MLA_NOPE + MLA_ROPE) ** -0.5 * LOG2E
MLA_QW = 2 * LANES
MLA_KVW = MLA_NOPE + MLA_V
MLA_IN_PAD = MLA_Q_RANK + MLA_KV_RANK + LANES


def _mla_proj_kernel(x_ref, g_ref, win_ref, qn_ref, kvn_ref, wuq_ref, wukv_ref,
                     cos_ref, sa_ref, sb_ref, q_ref, kv_ref, kpe_ref):
    h = _rms(x_ref[...], g_ref[...]).astype(BF16)
    proj = jnp.dot(h, win_ref[...], preferred_element_type=F32)
    cq = _rms(proj[:, :MLA_Q_RANK], qn_ref[...]).astype(BF16)
    ckv = _rms(proj[:, MLA_Q_RANK:MLA_Q_RANK + MLA_KV_RANK], kvn_ref[...]).astype(BF16)
    cos_t, sin_a, sin_b = cos_ref[...], sa_ref[...], sb_ref[...]
    kpe_ref[...] = _rope128(proj[:, MLA_Q_RANK + MLA_KV_RANK:], cos_t, sin_a, sin_b).astype(BF16)
    for hh in range(MLA_HEADS):
        c0 = hh * MLA_QW
        qh = jnp.dot(cq, wuq_ref[:, c0:c0 + MLA_QW], preferred_element_type=F32) * MLA_Q_SCALE
        q_ref[:, c0:c0 + MLA_NOPE] = qh[:, :MLA_NOPE].astype(BF16)
        q_ref[:, c0 + MLA_NOPE:c0 + MLA_QW] = _rope128(
            qh[:, MLA_NOPE:], cos_t, sin_a, sin_b).astype(BF16)
    n_kv = MLA_HEADS * MLA_KVW
    for c0 in range(0, n_kv, 1024):
        kv_ref[:, c0:c0 + 1024] = jnp.dot(
            ckv, wukv_ref[:, c0:c0 + 1024], preferred_element_type=F32).astype(BF16)


def _mla_proj(x, g, w_in, q_norm, w_uq, kv_norm, w_ukv, rope, tm=TM_MLA_PROJ):
    t, d = x.shape
    nq = MLA_HEADS * MLA_QW
    nkv = MLA_HEADS * MLA_KVW
    w_in_p = jnp.pad(w_in, ((0, 0), (0, MLA_IN_PAD - w_in.shape[1]))).astype(BF16)
    w_uq_p = jnp.pad(w_uq.reshape(MLA_Q_RANK, MLA_HEADS, MLA_NOPE + MLA_ROPE),
                     ((0, 0), (0, 0), (0, MLA_QW - MLA_NOPE - MLA_ROPE))
                     ).reshape(MLA_Q_RANK, nq).astype(BF16)
    w_ukv_p = jnp.swapaxes(w_ukv.reshape(MLA_KV_RANK, MLA_HEADS, 2, MLA_NOPE), 1, 2
                           ).reshape(MLA_KV_RANK, nkv).astype(BF16)
    row = lambda i: (i, 0)
    fixed = lambda i: (0, 0)
    return pl.pallas_call(
        _mla_proj_kernel,
        grid=(t // tm,),
        in_specs=[pl.BlockSpec((tm, d), row),
                  pl.BlockSpec((1, d), fixed),
                  pl.BlockSpec((d, MLA_IN_PAD), fixed),
                  pl.BlockSpec((1, MLA_Q_RANK), fixed),
                  pl.BlockSpec((1, MLA_KV_RANK), fixed),
                  pl.BlockSpec((MLA_Q_RANK, nq), fixed),
                  pl.BlockSpec((MLA_KV_RANK, nkv), fixed),
                  pl.BlockSpec((tm, LANES), row),
                  pl.BlockSpec((tm, LANES), row),
                  pl.BlockSpec((tm, LANES), row)],
        out_specs=[pl.BlockSpec((tm, nq), row),
                   pl.BlockSpec((tm, nkv), row),
                   pl.BlockSpec((tm, LANES), row)],
        out_shape=[jax.ShapeDtypeStruct((t, nq), BF16),
                   jax.ShapeDtypeStruct((t, nkv), BF16),
                   jax.ShapeDtypeStruct((t, LANES), BF16)],
        compiler_params=_params(("parallel",)),
        name="mla_proj",
    )(x, g.reshape(1, d), w_in_p, q_norm.reshape(1, -1), kv_norm.reshape(1, -1),
      w_uq_p, w_ukv_p, *rope)


def _mla_attn_kernel(q_ref, k_ref, kpe_ref, v_ref, o_ref, s_ref, m_ref,
                     *, hp, tq, nq):
    n = pl.program_id(0)
    kidx = lax.broadcasted_iota(jnp.int32, (tq, tq), 0)
    qidx = lax.broadcasted_iota(jnp.int32, (tq, tq), 1)
    causal = kidx <= qidx

    @pl.when(n == 0)
    def _():
        s_ref[...] = jnp.zeros_like(s_ref)
        m_ref[...] = jnp.zeros_like(m_ref)

    def scores(qi, slot):
        past = qi * tq
        kpe = kpe_ref[0:past + tq, :]
        for hh in range(hp):
            kc = jnp.concatenate(
                [k_ref[0:past + tq, hh * MLA_NOPE:(hh + 1) * MLA_NOPE], kpe], axis=1)
            q = q_ref[:, hh * MLA_QW:(hh + 1) * MLA_QW]
            s_diag = jnp.where(
                causal, lax.dot_general(kc[past:], q, NT_DIMS, preferred_element_type=F32),
                NEG_INF)
            s_ref[slot, hh, past:past + tq, :] = s_diag
            m = jnp.max(s_diag, axis=0, keepdims=True)
            if past:
                s_past = lax.dot_general(kc[:past], q, NT_DIMS, preferred_element_type=F32)
                s_ref[slot, hh, 0:past, :] = s_past
                m = jnp.maximum(m, jnp.max(s_past, axis=0, keepdims=True))
            m_ref[slot, hh] = m

    def attend(qi, slot):
        kv_len = (qi + 1) * tq
        for hh in range(hp):
            p = jnp.exp2(s_ref[slot, hh, 0:kv_len, :] - m_ref[slot, hh])
            l = jnp.sum(p, axis=0, keepdims=True)
            acc = lax.dot_general(v_ref[0:kv_len, hh * MLA_V:(hh + 1) * MLA_V],
                                  p.astype(BF16), TN_DIMS, preferred_element_type=F32)
            o_ref[:, hh * MLA_V:(hh + 1) * MLA_V] = (acc / l).T.astype(BF16)

    for r in range(nq):
        @pl.when(n % nq == r)
        def _(r=r):
            scores(r, r % 2)
            attend((r - 1) % nq, (r - 1) % 2)


def _mla_attn(q, kv, kpe, b, s, hp=MLA_ATTN_HEADS, tq=MLA_ATTN_TQ):
    assert MLA_NOPE == MLA_V
    t = b * s
    nq = s // tq
    assert nq % 2 == 0
    groups = MLA_HEADS // hp
    units = b * groups * nq

    def unit(n):
        return n // (groups * nq), (n // nq) % groups, n % nq

    def cur(n):
        return unit(jnp.minimum(n, units - 1))

    def last(n):
        return unit(jnp.maximum(n - 1, 0))

    def q_map(n):
        bb, g, i = cur(n)
        return bb * nq + i, g

    def k_map(n):
        bb, g, _ = cur(n)
        return bb, g

    def kpe_map(n):
        return cur(n)[0], 0

    def v_map(n):
        bb, g, _ = last(n)
        return bb, groups + g

    def o_map(n):
        bb, g, i = last(n)
        return bb * nq + i, g

    return pl.pallas_call(
        functools.partial(_mla_attn_kernel, hp=hp, tq=tq, nq=nq),
        grid=(units + 1,),
        in_specs=[pl.BlockSpec((tq, hp * MLA_QW), q_map),
                  pl.BlockSpec((s, hp * MLA_NOPE), k_map),
                  pl.BlockSpec((s, LANES), kpe_map),
                  pl.BlockSpec((s, hp * MLA_V), v_map)],
        out_specs=pl.BlockSpec((tq, hp * MLA_V), o_map),
        out_shape=jax.ShapeDtypeStruct((t, MLA_HEADS * MLA_V), BF16),
        scratch_shapes=[pltpu.VMEM((2, hp, s, tq), F32),
                        pltpu.VMEM((2, hp, 1, tq), F32)],
        compiler_params=_params(("arbitrary",), VMEM_MIB_LARGE),
        name="mla_attn",
    )(q, kv, kpe, kv)


def _t5_bucket_np(n):
    max_exact = REL_BUCKETS // 2
    nf = np.maximum(n, max_exact).astype(np.float32)
    large = max_exact + (np.log(nf / np.float32(max_exact))
                         / np.float32(math.log(REL_MAX_DIST / max_exact))
                         * np.float32(REL_BUCKETS - max_exact)).astype(np.int32)
    large = np.minimum(large, REL_BUCKETS - 1)
    return np.where(n < max_exact, n, large).astype(np.int32)


def _moba_bias_kernel(tab_ref, bucket_ref, o_ref):
    h = pl.program_id(0)
    bucket = bucket_ref[...]
    far = tab_ref[REL_BUCKETS - 1, h]
    acc = jnp.zeros(bucket.shape, F32)
    for k in range(REL_BUCKETS - 1):
        acc = jnp.where(bucket == k, tab_ref[k, h] - far, acc)
    o_ref[0] = acc * LOG2E


def _moba_bias(rel_bias):
    blk = MOBA_BLOCK
    key = np.arange(blk)[:, None]
    qry = np.arange(blk)[None, :]
    dist = np.stack([np.maximum(qry - key, 0), blk + qry - key])
    bucket = jnp.asarray(_t5_bucket_np(dist))
    return pl.pallas_call(
        _moba_bias_kernel,
        grid=(MOBA_HEADS,),
        in_specs=[pl.BlockSpec(memory_space=pltpu.SMEM),
                  pl.BlockSpec((2, blk, blk), lambda h: (0, 0, 0))],
        out_specs=pl.BlockSpec((1, 2, blk, blk), lambda h: (h, 0, 0, 0)),
        out_shape=jax.ShapeDtypeStruct((MOBA_HEADS, 2, blk, blk), F32),
        compiler_params=_params(("arbitrary",)),
        name="moba_bias",
    )(rel_bias, bucket)


def _moba_attn_kernel(q_ref, k_ref, v_ref, bias_ref, o_ref,
                      sel_ref, s_ref, m_ref, p_ref, *, hp, nb):
    n = pl.program_id(0)
    blk = MOBA_BLOCK
    dh = MOBA_HEAD_DIM
    s_len = nb * blk
    kidx = lax.broadcasted_iota(jnp.int32, (blk, blk), 0)
    qidx = lax.broadcasted_iota(jnp.int32, (blk, blk), 1)
    causal = kidx <= qidx

    @pl.when(n == 0)
    def _():
        s_ref[...] = jnp.zeros_like(s_ref)
        m_ref[...] = jnp.zeros_like(m_ref)

    def select_blocks():
        nidx = lax.broadcasted_iota(jnp.int32, (nb, s_len), 0)
        qblk = lax.broadcasted_iota(jnp.int32, (nb, s_len), 1) // blk
        past = nidx < qblk
        for hh in range(hp):
            cs = slice(hh * dh, (hh + 1) * dh)
            k_mean = jnp.concatenate(
                [jnp.sum(k_ref[n * blk:(n + 1) * blk, cs].astype(F32), axis=0, keepdims=True)
                 for n in range(nb)], axis=0) / blk
            h1 = k_mean.astype(BF16)
            r1 = k_mean - h1.astype(F32)
            h2 = r1.astype(BF16)
            h3 = (r1 - h2.astype(F32)).astype(BF16)
            g3 = lax.dot_general(jnp.concatenate([h1, h2, h3], axis=0), q_ref[:, cs], NT_DIMS,
                                 preferred_element_type=F32)
            gate = g3[0:nb] + g3[nb:2 * nb] + g3[2 * nb:3 * nb]
            gm = jnp.where(past, gate, NEG_INF)
            rank = jnp.zeros((nb, s_len), F32)
            for m in range(nb):
                gm_m = gm[m:m + 1, :]
                tie = jnp.where(gm_m == gm, jnp.where(nidx > m, 1.0, 0.0), 0.0)
                rank = rank + jnp.where(gm_m > gm, 1.0, tie)
            sel = jnp.where(rank < min(MOBA_TOPK, nb), jnp.where(past, 1.0, 0.0), 0.0)
            for qb in range(nb):
                sel_ref[hh, qb] = sel[:, qb * blk:(qb + 1) * blk]

    def scores(qi, slot):
        past = qi * blk
        for hh in range(hp):
            cs = slice(hh * dh, (hh + 1) * dh)
            q = q_ref[past:past + blk, cs]
            s_diag = lax.dot_general(k_ref[past:past + blk, cs], q, NT_DIMS,
                                     preferred_element_type=F32) + bias_ref[hh, 0]
            s_diag = jnp.where(causal, s_diag, NEG_INF)
            s_ref[slot, hh, past:past + blk, :] = s_diag
            m = jnp.max(s_diag, axis=0, keepdims=True)
            if past:
                s_past = lax.dot_general(k_ref[0:past, cs], q, NT_DIMS,
                                         preferred_element_type=F32)
                for j in range(qi):
                    s_j = s_past[j * blk:(j + 1) * blk]
                    if j == qi - 1:
                        s_j = s_j + bias_ref[hh, 1]
                    sel = sel_ref[hh, qi, j:j + 1, :]
                    s_j = jnp.where(sel > 0.5, s_j, NEG_INF)
                    s_ref[slot, hh, j * blk:(j + 1) * blk, :] = s_j
                    m = jnp.maximum(m, jnp.max(s_j, axis=0, keepdims=True))
            m_ref[slot, hh] = m

    def attend(qi, slot):
        kv_len = (qi + 1) * blk
        for hh in range(hp):
            cs = slice(hh * dh, (hh + 1) * dh)
            p = jnp.exp2(s_ref[slot, hh, 0:kv_len, :] - m_ref[slot, hh])
            l = jnp.sum(p, axis=0, keepdims=True)
            p_ref[hh, 0:kv_len, :] = p.astype(BF16)
            acc = lax.dot_general(v_ref[0:kv_len, cs], p_ref[hh, 0:kv_len, :], TN_DIMS,
                                  preferred_element_type=F32)
            o_ref[:, cs] = (acc / l).T.astype(BF16)

    for r in range(nb):
        @pl.when(n % nb == r)
        def _(r=r):
            if r == 0:
                select_blocks()
            scores(r, r % 2)
            attend((r - 1) % nb, (r - 1) % 2)


def _moba_attn(q, kv, bias, b, s, hp=MOBA_ATTN_HEADS):
    assert MOBA_BLOCK + 1 >= REL_MAX_DIST and s % MOBA_BLOCK == 0
    t = b * s
    blk = MOBA_BLOCK
    nb = s // blk
    assert nb % 2 == 0
    dh = MOBA_HEAD_DIM
    groups = MOBA_HEADS // hp
    units = b * groups * nb

    def unit(n):
        return n // (groups * nb), (n // nb) % groups, n % nb

    def cur(n):
        return unit(jnp.minimum(n, units - 1))

    def prev(n):
        return unit(jnp.maximum(n - 1, 0))

    def q_map(n):
        bb, g, _ = cur(n)
        return bb, g

    def bias_map(n):
        return cur(n)[1], 0, 0, 0

    def v_map(n):
        bb, g, _ = prev(n)
        return bb, groups + g

    def o_map(n):
        bb, g, i = prev(n)
        return bb * nb + i, g

    return pl.pallas_call(
        functools.partial(_moba_attn_kernel, hp=hp, nb=nb),
        grid=(units + 1,),
        in_specs=[pl.BlockSpec((s, hp * dh), q_map),
                  pl.BlockSpec((s, hp * dh), q_map),
                  pl.BlockSpec((s, hp * dh), v_map),
                  pl.BlockSpec((hp, 2, blk, blk), bias_map)],
        out_specs=pl.BlockSpec((blk, hp * dh), o_map),
        out_shape=jax.ShapeDtypeStruct((t, MOBA_HEADS * dh), BF16),
        scratch_shapes=[pltpu.VMEM((hp, nb, nb, blk), F32),
                        pltpu.VMEM((2, hp, s, blk), F32),
                        pltpu.VMEM((2, hp, 1, blk), F32),
                        pltpu.VMEM((hp, s, blk), BF16)],
        compiler_params=_params(("arbitrary",)),
        name="moba_attn",
    )(q, kv, kv, bias)


SSD_GN = SSD_GROUPS * SSD_STATE
SSD_GW = SSD_INNER // SSD_GROUPS


def _softplus(x):
    return jnp.maximum(x, 0.0) + jnp.log1p(jnp.exp(-jnp.abs(x)))


def _silu(x):
    return x * jax.nn.sigmoid(x)


def _expand_heads(v, e3):
    h1 = v.astype(BF16)
    r1 = v - h1.astype(F32)
    h2 = r1.astype(BF16)
    h3 = (r1 - h2.astype(F32)).astype(BF16)
    return jnp.dot(jnp.concatenate([h1, h2, h3], axis=1), e3, preferred_element_type=F32)


def _ssd_proj_kernel(x_ref, g_ref, w_ref, cw_ref, cb_ref, z_ref, xs_ref, bc_ref, dt_ref,
                     xpad_ref, *, tiles_per_seq):
    i = pl.program_id(0)
    tm = x_ref.shape[0]
    halo = SUBLANES
    c_xbc = SSD_INNER
    c_dt = SSD_INNER + SSD_CONV_DIM

    @pl.when(i % tiles_per_seq == 0)
    def _():
        xpad_ref[0:halo, :] = jnp.zeros((halo, SSD_CONV_DIM), F32)

    h = _rms(x_ref[...], g_ref[...]).astype(BF16)
    dt_ref[...] = jnp.dot(h, w_ref[:, c_dt:c_dt + LANES], preferred_element_type=F32)
    for c0 in range(0, SSD_CONV_DIM, 512):
        cs = slice(c0, c0 + 512)
        if c0 < SSD_INNER:
            z_ref[:, cs] = jnp.dot(h, w_ref[:, cs], preferred_element_type=F32)
        raw = jnp.dot(h, w_ref[:, c_xbc + c0:c_xbc + c0 + 512], preferred_element_type=F32)
        xpad_ref[halo:halo + tm, cs] = raw
        conv = cb_ref[:, cs] + cw_ref[SSD_CONV - 1:SSD_CONV, cs] * raw
        for k in range(SSD_CONV - 1):
            r0 = halo - (SSD_CONV - 1) + k
            conv = conv + cw_ref[k:k + 1, cs] * xpad_ref[r0:r0 + tm, cs]
        xpad_ref[0:halo, cs] = xpad_ref[tm:tm + halo, cs]
        xc = _silu(conv)
        if c0 < SSD_INNER:
            xs_ref[:, cs] = xc
        else:
            bc_ref[:, c0 - SSD_INNER:c0 - SSD_INNER + 512] = xc.astype(BF16)


def _ssd_proj(x, g, w_in, conv_w, conv_b, s, tm=TM_SSD_PROJ):
    t, d = x.shape
    assert SSD_CONV_DIM % 512 == 0 and SSD_INNER % 1024 == 0 and s % tm == 0
    w_pad = jnp.pad(w_in, ((0, 0), (0, LANES - SSD_HEADS))).astype(BF16)
    n = w_pad.shape[1]
    row = lambda i: (i, 0)
    fixed = lambda i: (0, 0)
    resident = lambda shape: pl.BlockSpec(shape, fixed, pipeline_mode=pl.Buffered(1))
    return pl.pallas_call(
        functools.partial(_ssd_proj_kernel, tiles_per_seq=s // tm),
        grid=(t // tm,),
        in_specs=[pl.BlockSpec((tm, d), row),
                  resident((1, d)),
                  resident((d, n)),
                  resident((SSD_CONV, SSD_CONV_DIM)),
                  resident((1, SSD_CONV_DIM))],
        out_specs=[pl.BlockSpec((tm, SSD_INNER), row),
                   pl.BlockSpec((tm, SSD_INNER), row),
                   pl.BlockSpec((tm, 2 * SSD_GN), row),
                   pl.BlockSpec((tm, LANES), row)],
        out_shape=[jax.ShapeDtypeStruct((t, SSD_INNER), F32),
                   jax.ShapeDtypeStruct((t, SSD_INNER), F32),
                   jax.ShapeDtypeStruct((t, 2 * SSD_GN), BF16),
                   jax.ShapeDtypeStruct((t, LANES), F32)],
        scratch_shapes=[pltpu.VMEM((tm + SUBLANES, SSD_CONV_DIM), F32)],
        compiler_params=_params(("arbitrary",), VMEM_MIB_LARGE),
        name="ssd_proj",
    )(x, g.reshape(1, d), w_pad, conv_w, conv_b.reshape(1, -1))


def _ssd_kernel(z_ref, xs_ref, bc_ref, dt_ref, dtb_ref, alog_ref, dexp_ref,
                nw_ref, e3_ref, o_ref, st_ref, *, chunks):
    @pl.when(pl.program_id(1) == 0)
    def _():
        st_ref[...] = jnp.zeros_like(st_ref)

    for cc in range(chunks):
        rows = slice(cc * SSD_CHUNK, (cc + 1) * SSD_CHUNK)
        _ssd_chunk(z_ref.at[rows], xs_ref.at[rows], bc_ref.at[rows], dt_ref.at[rows],
                   dtb_ref, alog_ref, dexp_ref, nw_ref, e3_ref, o_ref.at[rows], st_ref)


def _ssd_chunk(z_ref, xs_ref, bc_ref, dt_ref, dtb_ref, alog_ref, dexp_ref,
               nw_ref, e3_ref, o_ref, st_ref):
    L = SSD_CHUNK
    P = SSD_HEAD_DIM
    N = SSD_STATE
    xs = xs_ref[...]
    bm = bc_ref[:, :SSD_GN]
    cm = bc_ref[:, SSD_GN:]
    xs_b = xs.astype(BF16)

    dt = _softplus(dt_ref[...] + dtb_ref[...])
    a = -jnp.exp(alog_ref[...])
    row = lax.broadcasted_iota(jnp.int32, (L, L), 0)
    col = lax.broadcasted_iota(jnp.int32, (L, L), 1)
    tril = row >= col
    a_cs = jnp.dot(jnp.where(tril, 1.0, 0.0).astype(F32), dt * a,
                   precision=HIGHEST, preferred_element_type=F32)
    a_cs_t = a_cs.T
    dt_t = dt.T
    e3 = e3_ref[...]
    decay_exp = _expand_heads(jnp.exp(a_cs), e3)
    w_exp = _expand_heads(jnp.exp(a_cs[L - 1:L, :] - a_cs) * dt, e3)
    xw = (xs * w_exp).astype(BF16)
    lane = lax.broadcasted_iota(jnp.int32, (L, LANES), 1)

    ys = []
    for g in range(SSD_GROUPS):
        gs = slice(g * SSD_GW, (g + 1) * SSD_GW)
        b_g = bm[:, g * N:(g + 1) * N]
        c_g = cm[:, g * N:(g + 1) * N]
        cb = lax.dot_general(c_g, b_g, NT_DIMS, preferred_element_type=F32)
        state = st_ref[:, gs]
        y_off = jnp.dot(c_g, state.astype(BF16), preferred_element_type=F32) * decay_exp[:, gs]
        new_state = lax.dot_general(b_g, xw[:, gs], TN_DIMS, preferred_element_type=F32)
        st_ref[:, gs] = state * decay_exp[L - 1:L, gs] + new_state
        y_diag = []
        for pr in range(SSD_HEADS_PER_GROUP // 2):
            h0 = g * SSD_HEADS_PER_GROUP + 2 * pr
            mats = []
            for hr in (h0, h0 + 1):
                seg = a_cs[:, hr:hr + 1] - a_cs_t[hr:hr + 1, :]
                decay = jnp.exp(jnp.where(tril, seg, NEG_INF))
                mats.append((cb * decay * dt_t[hr:hr + 1, :]).astype(BF16))
            xp = xs_b[:, h0 * P:(h0 + 2) * P]
            zero = jnp.zeros_like(xp)
            x2 = jnp.concatenate([jnp.where(lane < P, xp, zero),
                                  jnp.where(lane >= P, xp, zero)], axis=0)
            y_diag.append(jnp.dot(jnp.concatenate(mats, axis=1), x2,
                                  preferred_element_type=F32))
        ys.append(jnp.concatenate(y_diag, axis=1) + y_off)
    y = jnp.concatenate(ys, axis=1) + dexp_ref[...] * xs
    gated = y * _silu(z_ref[...])
    outs = []
    for g in range(SSD_GROUPS):
        gg = gated[:, g * SSD_GW:(g + 1) * SSD_GW]
        ms = jnp.mean(gg * gg, axis=-1, keepdims=True)
        outs.append(gg * lax.rsqrt(ms + NORM_EPS))
    o_ref[...] = (jnp.concatenate(outs, axis=1) * nw_ref[...]).astype(BF16)


def _ssd_scan(z, xs, bc, dt, dt_bias, a_log, d_skip, norm_w, b, s,
              chunks=SSD_CHUNKS_PER_STEP):
    t = b * s
    L = chunks * SSD_CHUNK
    assert s % L == 0
    nc = s // L
    pad_h = lambda v: jnp.pad(v, (0, LANES - SSD_HEADS)).reshape(1, LANES)
    e = np.zeros((LANES, SSD_INNER), np.float32)
    for r in range(SSD_HEADS):
        e[r, r * SSD_HEAD_DIM:(r + 1) * SSD_HEAD_DIM] = 1.0
    e3 = jnp.asarray(np.concatenate([e, e, e], axis=0), dtype=BF16)
    d_exp = jnp.repeat(d_skip, SSD_HEAD_DIM).reshape(1, SSD_INNER)
    row = lambda bb, c: (bb * nc + c, 0)
    fixed = lambda bb, c: (0, 0)
    return pl.pallas_call(
        functools.partial(_ssd_kernel, chunks=chunks),
        grid=(b, nc),
        in_specs=[pl.BlockSpec((L, SSD_INNER), row),
                  pl.BlockSpec((L, SSD_INNER), row),
                  pl.BlockSpec((L, 2 * SSD_GN), row),
                  pl.BlockSpec((L, LANES), row),
                  pl.BlockSpec((1, LANES), fixed),
                  pl.BlockSpec((1, LANES), fixed),
                  pl.BlockSpec((1, SSD_INNER), fixed),
                  pl.BlockSpec((1, SSD_INNER), fixed),
                  pl.BlockSpec((3 * LANES, SSD_INNER), fixed)],
        out_specs=pl.BlockSpec((L, SSD_INNER), row),
        out_shape=jax.ShapeDtypeStruct((t, SSD_INNER), BF16),
        scratch_shapes=[pltpu.VMEM((SSD_STATE, SSD_INNER), F32)],
        compiler_params=_params(("parallel", "arbitrary")),
        name="ssd_scan",
    )(z, xs, bc, dt, pad_h(dt_bias), pad_h(a_log), d_exp, norm_w.reshape(1, -1), e3)


def _post_kernel(x_ref, a_ref, wo_ref, g_ref, w1_ref, w2_ref, gf_ref, o_ref,
                 *, final_norm, th):
    x1 = x_ref[...] + jnp.dot(a_ref[...], wo_ref[...], preferred_element_type=F32)
    h = _rms(x1, g_ref[...]).astype(BF16)
    y = x1
    for c0 in range(0, w1_ref.shape[1], th):
        u = jnp.maximum(
            jnp.dot(h, w1_ref[:, c0:c0 + th], preferred_element_type=F32), 0.0)
        y = y + jnp.dot((u * u).astype(BF16), w2_ref[c0:c0 + th, :],
                        preferred_element_type=F32)
    if final_norm:
        y = _rms(y, gf_ref[...])
    o_ref[...] = y


def _post(x, a, w_o, g, w1_all, w2_all, layer, g_final, final_norm,
          tm=TM_POST, th=TH_POST):
    t, d = x.shape
    ka = a.shape[1]
    hid = w1_all.shape[2]
    row = lambda i: (i, 0)
    fixed = lambda i: (0, 0)
    resident = lambda shape: pl.BlockSpec(shape, fixed, pipeline_mode=pl.Buffered(1))
    of_layer = lambda shape: pl.BlockSpec((None,) + shape, lambda i: (layer, 0, 0),
                                          pipeline_mode=pl.Buffered(1))
    return pl.pallas_call(
        functools.partial(_post_kernel, final_norm=final_norm, th=th),
        grid=(t // tm,),
        in_specs=[pl.BlockSpec((tm, d), row),
                  pl.BlockSpec((tm, ka), row),
                  resident((ka, d)),
                  resident((1, d)),
                  of_layer((d, hid)),
                  of_layer((hid, d)),
                  resident((1, d))],
        out_specs=pl.BlockSpec((tm, d), row),
        out_shape=jax.ShapeDtypeStruct((t, d), F32),
        compiler_params=_params(("parallel",), VMEM_MIB_LARGE),
        name="post_mlp",
    )(x, a, w_o.astype(BF16), g.reshape(1, d), w1_all, w2_all, g_final.reshape(1, d))


def _mla_mixer(x, g, rope, w_in, q_norm, w_uq, kv_norm, w_ukv, b, s):
    q, kv, kpe = _mla_proj(x, g, w_in, q_norm, w_uq, kv_norm, w_ukv, rope)
    return _mla_attn(q, kv, kpe, b, s)


def _moba_mixer(x, g, w_qkv, bias, b, s):
    nq = MOBA_HEADS * MOBA_HEAD_DIM
    q, kv = _norm_proj(x, g, w_qkv.astype(BF16), (nq, 2 * nq), (BF16, BF16),
                       (MOBA_HEAD_DIM ** -0.5 * LOG2E, None), tm=TM_NORM_PROJ)
    return _moba_attn(q, kv, bias, b, s)


def _ssd_mixer(x, g, w_in, conv_w, conv_b, dt_bias, a_log, d_skip, norm_w, b, s):
    z, xs, bc, dt = _ssd_proj(x, g, w_in, conv_w, conv_b, s)
    return _ssd_scan(z, xs, bc, dt, dt_bias, a_log, d_skip, norm_w, b, s)


def kernel(x, positions, norm_mix, norm_mlp, norm_final, rel_bias, mla_w_in, mla_q_norm, mla_w_uq, mla_kv_norm, mla_w_ukv, mla_w_o, moba_w_qkv, moba_w_o, ssd_w_in, ssd_conv_w, ssd_conv_b, ssd_dt_bias, ssd_a_log, ssd_d, ssd_norm, ssd_w_out, mlp_w1, mlp_w2):
    b, s, d = x.shape
    xf = x.reshape(b * s, d)
    rope = None
    bias = None
    w1_all = mlp_w1.astype(BF16)
    w2_all = mlp_w2.astype(BF16)
    for i in range(DEPTH):
        kind, j = i % N_MIXERS, i // N_MIXERS
        if kind == 0:
            if rope is None:
                rope = _rope_tables(positions)
            a = _mla_mixer(xf, norm_mix[i], rope, mla_w_in[j], mla_q_norm[j], mla_w_uq[j],
                           mla_kv_norm[j], mla_w_ukv[j], b, s)
            w_o = mla_w_o[j]
        elif kind == 1:
            if bias is None:
                bias = _moba_bias(rel_bias)
            a = _moba_mixer(xf, norm_mix[i], moba_w_qkv[j], bias, b, s)
            w_o = moba_w_o[j]
        else:
            a = _ssd_mixer(xf, norm_mix[i], ssd_w_in[j], ssd_conv_w[j], ssd_conv_b[j],
                           ssd_dt_bias[j], ssd_a_log[j], ssd_d[j], ssd_norm[j], b, s)
            w_o = ssd_w_out[j]
        xf = _post(xf, a, w_o, norm_mlp[i], w1_all, w2_all, i, norm_final,
                   final_norm=(i == DEPTH - 1))
    return xf.reshape(b, s, d)
```

```python
import functools
import math

import numpy as np
import jax
import jax.numpy as jnp
from jax import lax
from jax.experimental import pallas as pl
from jax.experimental.pallas import tpu as pltpu

F32 = jnp.float32
BF16 = jnp.bfloat16
HIGHEST = lax.Precision.HIGHEST

D_MODEL = 1024
DEPTH = 4
N_MIXERS = 3
NORM_EPS = 1e-6
NEG_INF = -1e30

MLA_HEADS = 16
MLA_Q_RANK = 256
MLA_KV_RANK = 256
MLA_NOPE = 128
MLA_ROPE = 64
MLA_V = 128
ROPE_BASE = 10000.0

MOBA_HEADS = 8
MOBA_HEAD_DIM = D_MODEL // MOBA_HEADS
MOBA_BLOCK = 256
MOBA_TOPK = 3
REL_BUCKETS = 32
REL_MAX_DIST = 128

SSD_INNER = 2 * D_MODEL
SSD_HEAD_DIM = 64
SSD_HEADS = SSD_INNER // SSD_HEAD_DIM
SSD_GROUPS = 2
SSD_HEADS_PER_GROUP = SSD_HEADS // SSD_GROUPS
SSD_STATE = 128
SSD_CONV = 4
SSD_CONV_DIM = SSD_INNER + 2 * SSD_GROUPS * SSD_STATE
SSD_CHUNK = 128

MLP_HIDDEN = 4 * D_MODEL

LANES = 128
SUBLANES = 8
MIB = 1024 * 1024

NT_DIMS = (((1,), (1,)), ((), ()))
TN_DIMS = (((0,), (0,)), ((), ()))

TM_NORM_PROJ = 1024
TM_MLA_PROJ = 512
TM_SSD_PROJ = 512
TM_POST = 512
TH_POST = 1024
MLA_ATTN_HEADS = 4
MLA_ATTN_TQ = 512
MOBA_ATTN_HEADS = 4
ATTN_KEY_CHUNK = 512
SSD_CHUNKS_PER_STEP = 4
VMEM_MIB = 48
VMEM_MIB_LARGE = 56


def _params(semantics, vmem_mib=VMEM_MIB):
    return pltpu.CompilerParams(dimension_semantics=semantics,
                                vmem_limit_bytes=vmem_mib * MIB)


def _rms(x, g):
    ms = jnp.mean(x * x, axis=-1, keepdims=True)
    return x * lax.rsqrt(ms + NORM_EPS) * g


def _norm_proj_kernel(x_ref, g_ref, w_ref, *o_refs, splits, scales, col_chunk):
    h = _rms(x_ref[...], g_ref[...]).astype(BF16)
    off = 0
    for o_ref, n, scale in zip(o_refs, splits, scales):
        for c0 in range(0, n, col_chunk):
            c1 = min(c0 + col_chunk, n)
            r = jnp.dot(h, w_ref[:, off + c0:off + c1], preferred_element_type=F32)
            if scale is not None:
                r = r * scale
            o_ref[:, c0:c1] = r.astype(o_ref.dtype)
        off += n


def _norm_proj(x, g, w, splits, dtypes, scales, tm):
    t, d = x.shape
    n = w.shape[1]
    assert sum(splits) == n and t % tm == 0
    return pl.pallas_call(
        functools.partial(_norm_proj_kernel, splits=tuple(splits), scales=tuple(scales),
                          col_chunk=1024),
        grid=(t // tm,),
        in_specs=[pl.BlockSpec((tm, d), lambda i: (i, 0)),
                  pl.BlockSpec((1, d), lambda i: (0, 0)),
                  pl.BlockSpec((d, n), lambda i: (0, 0))],
        out_specs=[pl.BlockSpec((tm, s), lambda i: (i, 0)) for s in splits],
        out_shape=[jax.ShapeDtypeStruct((t, s), dt) for s, dt in zip(splits, dtypes)],
        compiler_params=_params(("parallel",), VMEM_MIB_LARGE),
        name="norm_proj",
    )(x, g.reshape(1, d), w)


def _rope_table_kernel(pos_ref, freq_ref, cos_ref, sin_ref):
    ang = pos_ref[...].astype(F32) * freq_ref[...]
    cos_ref[...] = jnp.cos(ang)
    sin_ref[...] = jnp.sin(ang)


def _rope_tables(positions):
    b, s = positions.shape
    t = b * s
    half = MLA_ROPE // 2
    per_row = LANES // half
    rows = t // per_row
    inv_freq = ROPE_BASE ** (-(jnp.arange(0, MLA_ROPE, 2, dtype=F32) / MLA_ROPE))
    pos_rep = jnp.repeat(positions.reshape(rows, per_row), half, axis=1)
    freq = jnp.tile(inv_freq, per_row).reshape(1, LANES)
    tr = 512
    cos, sin = pl.pallas_call(
        _rope_table_kernel,
        grid=(rows // tr,),
        in_specs=[pl.BlockSpec((tr, LANES), lambda i: (i, 0)),
                  pl.BlockSpec((1, LANES), lambda i: (0, 0))],
        out_specs=[pl.BlockSpec((tr, LANES), lambda i: (i, 0))] * 2,
        out_shape=[jax.ShapeDtypeStruct((rows, LANES), F32)] * 2,
        compiler_params=_params(("parallel",)),
        name="rope_tables",
    )(pos_rep, freq)
    cos = cos.reshape(t, half)
    sin = sin.reshape(t, half)
    zeros = lambda n: jnp.zeros((t, n), F32)
    cos_t = jnp.concatenate([cos, cos, zeros(LANES - 2 * half)], axis=1)
    sin_a = jnp.concatenate([-sin, zeros(LANES - half)], axis=1)
    sin_b = jnp.concatenate([zeros(half), sin, zeros(LANES - 2 * half)], axis=1)
    return cos_t, sin_a, sin_b


def _rope128(v, cos_t, sin_a, sin_b):
    half = MLA_ROPE // 2
    return (v * cos_t + pltpu.roll(v, LANES - half, 1) * sin_a
            + pltpu.roll(v, half, 1) * sin_b)


LOG2E = math.log2(math.e)
MLA_Q_SCALE = (MLA_NOPE + MLA_ROPE) ** -0.5 * LOG2E
MLA_QW = 2 * LANES
MLA_KVW = MLA_NOPE + MLA_V
MLA_IN_PAD = MLA_Q_RANK + MLA_KV_RANK + LANES


def _mla_proj_kernel(x_ref, g_ref, win_ref, qn_ref, kvn_ref, wuq_ref, wukv_ref,
                     cos_ref, sa_ref, sb_ref, q_ref, kv_ref, kpe_ref):
    h = _rms(x_ref[...], g_ref[...]).astype(BF16)
    proj = jnp.dot(h, win_ref[...], preferred_element_type=F32)
    cq = _rms(proj[:, :MLA_Q_RANK], qn_ref[...]).astype(BF16)
    ckv = _rms(proj[:, MLA_Q_RANK:MLA_Q_RANK + MLA_KV_RANK], kvn_ref[...]).astype(BF16)
    cos_t, sin_a, sin_b = cos_ref[...], sa_ref[...], sb_ref[...]
    kpe_ref[...] = _rope128(proj[:, MLA_Q_RANK + MLA_KV_RANK:], cos_t, sin_a, sin_b).astype(BF16)
    for hh in range(MLA_HEADS):
        c0 = hh * MLA_QW
        qh = jnp.dot(cq, wuq_ref[:, c0:c0 + MLA_QW], preferred_element_type=F32) * MLA_Q_SCALE
        q_ref[:, c0:c0 + MLA_NOPE] = qh[:, :MLA_NOPE].astype(BF16)
        q_ref[:, c0 + MLA_NOPE:c0 + MLA_QW] = _rope128(
            qh[:, MLA_NOPE:], cos_t, sin_a, sin_b).astype(BF16)
    n_kv = MLA_HEADS * MLA_KVW
    for c0 in range(0, n_kv, 1024):
        kv_ref[:, c0:c0 + 1024] = jnp.dot(
            ckv, wukv_ref[:, c0:c0 + 1024], preferred_element_type=F32).astype(BF16)


def _mla_proj(x, g, w_in, q_norm, w_uq, kv_norm, w_ukv, rope, tm=TM_MLA_PROJ):
    t, d = x.shape
    nq = MLA_HEADS * MLA_QW
    nkv = MLA_HEADS * MLA_KVW
    w_in_p = jnp.pad(w_in, ((0, 0), (0, MLA_IN_PAD - w_in.shape[1]))).astype(BF16)
    w_uq_p = jnp.pad(w_uq.reshape(MLA_Q_RANK, MLA_HEADS, MLA_NOPE + MLA_ROPE),
                     ((0, 0), (0, 0), (0, MLA_QW - MLA_NOPE - MLA_ROPE))
                     ).reshape(MLA_Q_RANK, nq).astype(BF16)
    w_ukv_p = jnp.swapaxes(w_ukv.reshape(MLA_KV_RANK, MLA_HEADS, 2, MLA_NOPE), 1, 2
                           ).reshape(MLA_KV_RANK, nkv).astype(BF16)
    row = lambda i: (i, 0)
    fixed = lambda i: (0, 0)
    return pl.pallas_call(
        _mla_proj_kernel,
        grid=(t // tm,),
        in_specs=[pl.BlockSpec((tm, d), row),
                  pl.BlockSpec((1, d), fixed),
                  pl.BlockSpec((d, MLA_IN_PAD), fixed),
                  pl.BlockSpec((1, MLA_Q_RANK), fixed),
                  pl.BlockSpec((1, MLA_KV_RANK), fixed),
                  pl.BlockSpec((MLA_Q_RANK, nq), fixed),
                  pl.BlockSpec((MLA_KV_RANK, nkv), fixed),
                  pl.BlockSpec((tm, LANES), row),
                  pl.BlockSpec((tm, LANES), row),
                  pl.BlockSpec((tm, LANES), row)],
        out_specs=[pl.BlockSpec((tm, nq), row),
                   pl.BlockSpec((tm, nkv), row),
                   pl.BlockSpec((tm, LANES), row)],
        out_shape=[jax.ShapeDtypeStruct((t, nq), BF16),
                   jax.ShapeDtypeStruct((t, nkv), BF16),
                   jax.ShapeDtypeStruct((t, LANES), BF16)],
        compiler_params=_params(("parallel",)),
        name="mla_proj",
    )(x, g.reshape(1, d), w_in_p, q_norm.reshape(1, -1), kv_norm.reshape(1, -1),
      w_uq_p, w_ukv_p, *rope)


def _mla_attn_kernel(q_ref, k_ref, kpe_ref, v_ref, o_ref, s_ref, m_ref,
                     *, hp, tq, nq):
    n = pl.program_id(0)
    kidx = lax.broadcasted_iota(jnp.int32, (tq, tq), 0)
    qidx = lax.broadcasted_iota(jnp.int32, (tq, tq), 1)
    causal = kidx <= qidx

    @pl.when(n == 0)
    def _():
        s_ref[...] = jnp.zeros_like(s_ref)
        m_ref[...] = jnp.zeros_like(m_ref)

    def scores(qi, slot):
        past = qi * tq
        kpe = kpe_ref[0:past + tq, :]
        for hh in range(hp):
            kc = jnp.concatenate(
                [k_ref[0:past + tq, hh * MLA_NOPE:(hh + 1) * MLA_NOPE], kpe], axis=1)
            q = q_ref[:, hh * MLA_QW:(hh + 1) * MLA_QW]
            s_diag = jnp.where(
                causal, lax.dot_general(kc[past:], q, NT_DIMS, preferred_element_type=F32),
                NEG_INF)
            s_ref[slot, hh, past:past + tq, :] = s_diag
            m = jnp.max(s_diag, axis=0, keepdims=True)
            if past:
                s_past = lax.dot_general(kc[:past], q, NT_DIMS, preferred_element_type=F32)
                s_ref[slot, hh, 0:past, :] = s_past
                m = jnp.maximum(m, jnp.max(s_past, axis=0, keepdims=True))
            m_ref[slot, hh] = m

    def attend(qi, slot):
        kv_len = (qi + 1) * tq
        for hh in range(hp):
            m = m_ref[slot, hh]
            l = jnp.zeros((1, tq), F32)
            acc = jnp.zeros((MLA_V, tq), F32)
            for k0 in range(0, kv_len, ATTN_KEY_CHUNK):
                k1 = min(k0 + ATTN_KEY_CHUNK, kv_len)
                p = jnp.exp2(s_ref[slot, hh, k0:k1, :] - m)
                l = l + jnp.sum(p, axis=0, keepdims=True)
                acc = acc + lax.dot_general(v_ref[k0:k1, hh * MLA_V:(hh + 1) * MLA_V],
                                            p.astype(BF16), TN_DIMS,
                                            preferred_element_type=F32)
            o_ref[:, hh * MLA_V:(hh + 1) * MLA_V] = (acc / l).T.astype(BF16)

    for r in range(nq):
        @pl.when(n % nq == r)
        def _(r=r):
            scores(r, r % 2)
            attend((r - 1) % nq, (r - 1) % 2)


def _mla_attn(q, kv, kpe, b, s, hp=MLA_ATTN_HEADS, tq=MLA_ATTN_TQ):
    assert MLA_NOPE == MLA_V
    t = b * s
    nq = s // tq
    assert nq % 2 == 0
    groups = MLA_HEADS // hp
    units = b * groups * nq

    def unit(n):
        return n // (groups * nq), (n // nq) % groups, n % nq

    def cur(n):
        return unit(jnp.minimum(n, units - 1))

    def last(n):
        return unit(jnp.maximum(n - 1, 0))

    def q_map(n):
        bb, g, i = cur(n)
        return bb * nq + i, g

    def k_map(n):
        bb, g, _ = cur(n)
        return bb, g

    def kpe_map(n):
        return cur(n)[0], 0

    def v_map(n):
        bb, g, _ = last(n)
        return bb, groups + g

    def o_map(n):
        bb, g, i = last(n)
        return bb * nq + i, g

    return pl.pallas_call(
        functools.partial(_mla_attn_kernel, hp=hp, tq=tq, nq=nq),
        grid=(units + 1,),
        in_specs=[pl.BlockSpec((tq, hp * MLA_QW), q_map),
                  pl.BlockSpec((s, hp * MLA_NOPE), k_map),
                  pl.BlockSpec((s, LANES), kpe_map),
                  pl.BlockSpec((s, hp * MLA_V), v_map)],
        out_specs=pl.BlockSpec((tq, hp * MLA_V), o_map),
        out_shape=jax.ShapeDtypeStruct((t, MLA_HEADS * MLA_V), BF16),
        scratch_shapes=[pltpu.VMEM((2, hp, s, tq), F32),
                        pltpu.VMEM((2, hp, 1, tq), F32)],
        compiler_params=_params(("arbitrary",), VMEM_MIB_LARGE),
        name="mla_attn",
    )(q, kv, kpe, kv)


def _t5_bucket_np(n):
    max_exact = REL_BUCKETS // 2
    nf = np.maximum(n, max_exact).astype(np.float32)
    large = max_exact + (np.log(nf / np.float32(max_exact))
                         / np.float32(math.log(REL_MAX_DIST / max_exact))
                         * np.float32(REL_BUCKETS - max_exact)).astype(np.int32)
    large = np.minimum(large, REL_BUCKETS - 1)
    return np.where(n < max_exact, n, large).astype(np.int32)


def _moba_bias_kernel(tab_ref, bucket_ref, o_ref):
    h = pl.program_id(0)
    bucket = bucket_ref[...]
    far = tab_ref[REL_BUCKETS - 1, h]
    acc = jnp.zeros(bucket.shape, F32)
    for k in range(REL_BUCKETS - 1):
        acc = jnp.where(bucket == k, tab_ref[k, h] - far, acc)
    o_ref[0] = acc * LOG2E


def _moba_bias(rel_bias):
    blk = MOBA_BLOCK
    key = np.arange(blk)[:, None]
    qry = np.arange(blk)[None, :]
    dist = np.stack([np.maximum(qry - key, 0), blk + qry - key])
    bucket = jnp.asarray(_t5_bucket_np(dist))
    return pl.pallas_call(
        _moba_bias_kernel,
        grid=(MOBA_HEADS,),
        in_specs=[pl.BlockSpec(memory_space=pltpu.SMEM),
                  pl.BlockSpec((2, blk, blk), lambda h: (0, 0, 0))],
        out_specs=pl.BlockSpec((1, 2, blk, blk), lambda h: (h, 0, 0, 0)),
        out_shape=jax.ShapeDtypeStruct((MOBA_HEADS, 2, blk, blk), F32),
        compiler_params=_params(("arbitrary",)),
        name="moba_bias",
    )(rel_bias, bucket)


def _moba_attn_kernel(q_ref, k_ref, v_ref, bias_ref, o_ref,
                      sel_ref, s_ref, m_ref, p_ref, *, hp, nb):
    n = pl.program_id(0)
    blk = MOBA_BLOCK
    dh = MOBA_HEAD_DIM
    s_len = nb * blk
    kidx = lax.broadcasted_iota(jnp.int32, (blk, blk), 0)
    qidx = lax.broadcasted_iota(jnp.int32, (blk, blk), 1)
    causal = kidx <= qidx

    @pl.when(n == 0)
    def _():
        s_ref[...] = jnp.zeros_like(s_ref)
        m_ref[...] = jnp.zeros_like(m_ref)

    def select_blocks():
        nidx = lax.broadcasted_iota(jnp.int32, (nb, s_len), 0)
        qblk = lax.broadcasted_iota(jnp.int32, (nb, s_len), 1) // blk
        past = nidx < qblk
        for hh in range(hp):
            cs = slice(hh * dh, (hh + 1) * dh)
            k_mean = jnp.concatenate(
                [jnp.sum(k_ref[n * blk:(n + 1) * blk, cs].astype(F32), axis=0, keepdims=True)
                 for n in range(nb)], axis=0) / blk
            h1 = k_mean.astype(BF16)
            r1 = k_mean - h1.astype(F32)
            h2 = r1.astype(BF16)
            h3 = (r1 - h2.astype(F32)).astype(BF16)
            g3 = lax.dot_general(jnp.concatenate([h1, h2, h3], axis=0), q_ref[:, cs], NT_DIMS,
                                 preferred_element_type=F32)
            gate = g3[0:nb] + g3[nb:2 * nb] + g3[2 * nb:3 * nb]
            gm = jnp.where(past, gate, NEG_INF)
            rank = jnp.zeros((nb, s_len), F32)
            for m in range(nb):
                gm_m = gm[m:m + 1, :]
                tie = jnp.where(gm_m == gm, jnp.where(nidx > m, 1.0, 0.0), 0.0)
                rank = rank + jnp.where(gm_m > gm, 1.0, tie)
            sel = jnp.where(rank < min(MOBA_TOPK, nb), jnp.where(past, 1.0, 0.0), 0.0)
            for qb in range(nb):
                sel_ref[hh, qb] = sel[:, qb * blk:(qb + 1) * blk]

    def scores(qi, slot):
        past = qi * blk
        for hh in range(hp):
            cs = slice(hh * dh, (hh + 1) * dh)
            q = q_ref[past:past + blk, cs]
            s_diag = lax.dot_general(k_ref[past:past + blk, cs], q, NT_DIMS,
                                     preferred_element_type=F32) + bias_ref[hh, 0]
            s_diag = jnp.where(causal, s_diag, NEG_INF)
            s_ref[slot, hh, past:past + blk, :] = s_diag
            m = jnp.max(s_diag, axis=0, keepdims=True)
            if past:
                s_past = lax.dot_general(k_ref[0:past, cs], q, NT_DIMS,
                                         preferred_element_type=F32)
                for j in range(qi):
                    s_j = s_past[j * blk:(j + 1) * blk]
                    if j == qi - 1:
                        s_j = s_j + bias_ref[hh, 1]
                    sel = sel_ref[hh, qi, j:j + 1, :]
                    s_j = jnp.where(sel > 0.5, s_j, NEG_INF)
                    s_ref[slot, hh, j * blk:(j + 1) * blk, :] = s_j
                    m = jnp.maximum(m, jnp.max(s_j, axis=0, keepdims=True))
            m_ref[slot, hh] = m

    def attend(qi, slot):
        kv_len = (qi + 1) * blk
        for hh in range(hp):
            cs = slice(hh * dh, (hh + 1) * dh)
            m = m_ref[slot, hh]
            l = jnp.zeros((1, blk), F32)
            acc = jnp.zeros((dh, blk), F32)
            for k0 in range(0, kv_len, ATTN_KEY_CHUNK):
                k1 = min(k0 + ATTN_KEY_CHUNK, kv_len)
                p = jnp.exp2(s_ref[slot, hh, k0:k1, :] - m)
                l = l + jnp.sum(p, axis=0, keepdims=True)
                acc = acc + lax.dot_general(v_ref[k0:k1, cs], p.astype(BF16), TN_DIMS,
                                            preferred_element_type=F32)
            o_ref[:, cs] = (acc / l).T.astype(BF16)

    for r in range(nb):
        @pl.when(n % nb == r)
        def _(r=r):
            if r == 0:
                select_blocks()
            scores(r, r % 2)
            attend((r - 1) % nb, (r - 1) % 2)


def _moba_attn(q, kv, bias, b, s, hp=MOBA_ATTN_HEADS):
    assert MOBA_BLOCK + 1 >= REL_MAX_DIST and s % MOBA_BLOCK == 0
    t = b * s
    blk = MOBA_BLOCK
    nb = s // blk
    assert nb % 2 == 0
    dh = MOBA_HEAD_DIM
    groups = MOBA_HEADS // hp
    units = b * groups * nb

    def unit(n):
        return n // (groups * nb), (n // nb) % groups, n % nb

    def cur(n):
        return unit(jnp.minimum(n, units - 1))

    def prev(n):
        return unit(jnp.maximum(n - 1, 0))

    def q_map(n):
        bb, g, _ = cur(n)
        return bb, g

    def bias_map(n):
        return cur(n)[1], 0, 0, 0

    def v_map(n):
        bb, g, _ = prev(n)
        return bb, groups + g

    def o_map(n):
        bb, g, i = prev(n)
        return bb * nb + i, g

    return pl.pallas_call(
        functools.partial(_moba_attn_kernel, hp=hp, nb=nb),
        grid=(units + 1,),
        in_specs=[pl.BlockSpec((s, hp * dh), q_map),
                  pl.BlockSpec((s, hp * dh), q_map),
                  pl.BlockSpec((s, hp * dh), v_map),
                  pl.BlockSpec((hp, 2, blk, blk), bias_map)],
        out_specs=pl.BlockSpec((blk, hp * dh), o_map),
        out_shape=jax.ShapeDtypeStruct((t, MOBA_HEADS * dh), BF16),
        scratch_shapes=[pltpu.VMEM((hp, nb, nb, blk), F32),
                        pltpu.VMEM((2, hp, s, blk), F32),
                        pltpu.VMEM((2, hp, 1, blk), F32),
                        pltpu.VMEM((hp, s, blk), BF16)],
        compiler_params=_params(("arbitrary",)),
        name="moba_attn",
    )(q, kv, kv, bias)


SSD_GN = SSD_GROUPS * SSD_STATE
SSD_GW = SSD_INNER // SSD_GROUPS


def _softplus(x):
    return jnp.maximum(x, 0.0) + jnp.log1p(jnp.exp(-jnp.abs(x)))


def _silu(x):
    return x * jax.nn.sigmoid(x)


def _expand_heads(v, e3):
    h1 = v.astype(BF16)
    r1 = v - h1.astype(F32)
    h2 = r1.astype(BF16)
    h3 = (r1 - h2.astype(F32)).astype(BF16)
    return jnp.dot(jnp.concatenate([h1, h2, h3], axis=1), e3, preferred_element_type=F32)


def _ssd_proj_kernel(x_ref, g_ref, w_ref, cw_ref, cb_ref, z_ref, xs_ref, bc_ref, dt_ref,
                     xpad_ref, *, tiles_per_seq):
    i = pl.program_id(0)
    tm = x_ref.shape[0]
    halo = SUBLANES
    c_xbc = SSD_INNER
    c_dt = SSD_INNER + SSD_CONV_DIM

    @pl.when(i % tiles_per_seq == 0)
    def _():
        xpad_ref[0:halo, :] = jnp.zeros((halo, SSD_CONV_DIM), F32)

    h = _rms(x_ref[...], g_ref[...]).astype(BF16)
    dt_ref[...] = jnp.dot(h, w_ref[:, c_dt:c_dt + LANES], preferred_element_type=F32)
    for c0 in range(0, SSD_CONV_DIM, 512):
        cs = slice(c0, c0 + 512)
        if c0 < SSD_INNER:
            z_ref[:, cs] = jnp.dot(h, w_ref[:, cs], preferred_element_type=F32)
        raw = jnp.dot(h, w_ref[:, c_xbc + c0:c_xbc + c0 + 512], preferred_element_type=F32)
        xpad_ref[halo:halo + tm, cs] = raw
        conv = cb_ref[:, cs] + cw_ref[SSD_CONV - 1:SSD_CONV, cs] * raw
        for k in range(SSD_CONV - 1):
            r0 = halo - (SSD_CONV - 1) + k
            conv = conv + cw_ref[k:k + 1, cs] * xpad_ref[r0:r0 + tm, cs]
        xpad_ref[0:halo, cs] = xpad_ref[tm:tm + halo, cs]
        xc = _silu(conv)
        if c0 < SSD_INNER:
            xs_ref[:, cs] = xc
        else:
            bc_ref[:, c0 - SSD_INNER:c0 - SSD_INNER + 512] = xc.astype(BF16)


def _ssd_proj(x, g, w_in, conv_w, conv_b, s, tm=TM_SSD_PROJ):
    t, d = x.shape
    assert SSD_CONV_DIM % 512 == 0 and SSD_INNER % 1024 == 0 and s % tm == 0
    w_pad = jnp.pad(w_in, ((0, 0), (0, LANES - SSD_HEADS))).astype(BF16)
    n = w_pad.shape[1]
    row = lambda i: (i, 0)
    fixed = lambda i: (0, 0)
    resident = lambda shape: pl.BlockSpec(shape, fixed, pipeline_mode=pl.Buffered(1))
    return pl.pallas_call(
        functools.partial(_ssd_proj_kernel, tiles_per_seq=s // tm),
        grid=(t // tm,),
        in_specs=[pl.BlockSpec((tm, d), row),
                  resident((1, d)),
                  resident((d, n)),
                  resident((SSD_CONV, SSD_CONV_DIM)),
                  resident((1, SSD_CONV_DIM))],
        out_specs=[pl.BlockSpec((tm, SSD_INNER), row),
                   pl.BlockSpec((tm, SSD_INNER), row),
                   pl.BlockSpec((tm, 2 * SSD_GN), row),
                   pl.BlockSpec((tm, LANES), row)],
        out_shape=[jax.ShapeDtypeStruct((t, SSD_INNER), F32),
                   jax.ShapeDtypeStruct((t, SSD_INNER), F32),
                   jax.ShapeDtypeStruct((t, 2 * SSD_GN), BF16),
                   jax.ShapeDtypeStruct((t, LANES), F32)],
        scratch_shapes=[pltpu.VMEM((tm + SUBLANES, SSD_CONV_DIM), F32)],
        compiler_params=_params(("arbitrary",), VMEM_MIB_LARGE),
        name="ssd_proj",
    )(x, g.reshape(1, d), w_pad, conv_w, conv_b.reshape(1, -1))


def _ssd_kernel(z_ref, xs_ref, bc_ref, dt_ref, dtb_ref, alog_ref, dexp_ref,
                nw_ref, e3_ref, o_ref, st_ref, *, chunks):
    @pl.when(pl.program_id(1) == 0)
    def _():
        st_ref[...] = jnp.zeros_like(st_ref)

    for cc in range(chunks):
        rows = slice(cc * SSD_CHUNK, (cc + 1) * SSD_CHUNK)
        _ssd_chunk(z_ref.at[rows], xs_ref.at[rows], bc_ref.at[rows], dt_ref.at[rows],
                   dtb_ref, alog_ref, dexp_ref, nw_ref, e3_ref, o_ref.at[rows], st_ref)


def _ssd_chunk(z_ref, xs_ref, bc_ref, dt_ref, dtb_ref, alog_ref, dexp_ref,
               nw_ref, e3_ref, o_ref, st_ref):
    L = SSD_CHUNK
    P = SSD_HEAD_DIM
    N = SSD_STATE
    xs = xs_ref[...]
    bm = bc_ref[:, :SSD_GN]
    cm = bc_ref[:, SSD_GN:]
    xs_b = xs.astype(BF16)

    dt = _softplus(dt_ref[...] + dtb_ref[...])
    a = -jnp.exp(alog_ref[...])
    row = lax.broadcasted_iota(jnp.int32, (L, L), 0)
    col = lax.broadcasted_iota(jnp.int32, (L, L), 1)
    tril = row >= col
    a_cs = jnp.dot(jnp.where(tril, 1.0, 0.0).astype(F32), dt * a,
                   precision=HIGHEST, preferred_element_type=F32)
    a_cs_t = a_cs.T
    dt_t = dt.T
    e3 = e3_ref[...]
    decay_exp = _expand_heads(jnp.exp(a_cs), e3)
    w_exp = _expand_heads(jnp.exp(a_cs[L - 1:L, :] - a_cs) * dt, e3)
    xw = (xs * w_exp).astype(BF16)
    lane = lax.broadcasted_iota(jnp.int32, (L, LANES), 1)

    ys = []
    for g in range(SSD_GROUPS):
        gs = slice(g * SSD_GW, (g + 1) * SSD_GW)
        b_g = bm[:, g * N:(g + 1) * N]
        c_g = cm[:, g * N:(g + 1) * N]
        cb = lax.dot_general(c_g, b_g, NT_DIMS, preferred_element_type=F32)
        state = st_ref[:, gs]
        y_off = jnp.dot(c_g, state.astype(BF16), preferred_element_type=F32) * decay_exp[:, gs]
        new_state = lax.dot_general(b_g, xw[:, gs], TN_DIMS, preferred_element_type=F32)
        st_ref[:, gs] = state * decay_exp[L - 1:L, gs] + new_state
        y_diag = []
        for pr in range(SSD_HEADS_PER_GROUP // 2):
            h0 = g * SSD_HEADS_PER_GROUP + 2 * pr
            mats = []
            for hr in (h0, h0 + 1):
                seg = a_cs[:, hr:hr + 1] - a_cs_t[hr:hr + 1, :]
                decay = jnp.exp(jnp.where(tril, seg, NEG_INF))
                mats.append((cb * decay * dt_t[hr:hr + 1, :]).astype(BF16))
            xp = xs_b[:, h0 * P:(h0 + 2) * P]
            zero = jnp.zeros_like(xp)
            x2 = jnp.concatenate([jnp.where(lane < P, xp, zero),
                                  jnp.where(lane >= P, xp, zero)], axis=0)
            y_diag.append(jnp.dot(jnp.concatenate(mats, axis=1), x2,
                                  preferred_element_type=F32))
        ys.append(jnp.concatenate(y_diag, axis=1) + y_off)
    y = jnp.concatenate(ys, axis=1) + dexp_ref[...] * xs
    gated = y * _silu(z_ref[...])
    outs = []
    for g in range(SSD_GROUPS):
        gg = gated[:, g * SSD_GW:(g + 1) * SSD_GW]
        ms = jnp.mean(gg * gg, axis=-1, keepdims=True)
        outs.append(gg * lax.rsqrt(ms + NORM_EPS))
    o_ref[...] = (jnp.concatenate(outs, axis=1) * nw_ref[...]).astype(BF16)


def _ssd_scan(z, xs, bc, dt, dt_bias, a_log, d_skip, norm_w, b, s,
              chunks=SSD_CHUNKS_PER_STEP):
    t = b * s
    L = chunks * SSD_CHUNK
    assert s % L == 0
    nc = s // L
    pad_h = lambda v: jnp.pad(v, (0, LANES - SSD_HEADS)).reshape(1, LANES)
    e = np.zeros((LANES, SSD_INNER), np.float32)
    for r in range(SSD_HEADS):
        e[r, r * SSD_HEAD_DIM:(r + 1) * SSD_HEAD_DIM] = 1.0
    e3 = jnp.asarray(np.concatenate([e, e, e], axis=0), dtype=BF16)
    d_exp = jnp.repeat(d_skip, SSD_HEAD_DIM).reshape(1, SSD_INNER)
    row = lambda bb, c: (bb * nc + c, 0)
    fixed = lambda bb, c: (0, 0)
    return pl.pallas_call(
        functools.partial(_ssd_kernel, chunks=chunks),
        grid=(b, nc),
        in_specs=[pl.BlockSpec((L, SSD_INNER), row),
                  pl.BlockSpec((L, SSD_INNER), row),
                  pl.BlockSpec((L, 2 * SSD_GN), row),
                  pl.BlockSpec((L, LANES), row),
                  pl.BlockSpec((1, LANES), fixed),
                  pl.BlockSpec((1, LANES), fixed),
                  pl.BlockSpec((1, SSD_INNER), fixed),
                  pl.BlockSpec((1, SSD_INNER), fixed),
                  pl.BlockSpec((3 * LANES, SSD_INNER), fixed)],
        out_specs=pl.BlockSpec((L, SSD_INNER), row),
        out_shape=jax.ShapeDtypeStruct((t, SSD_INNER), BF16),
        scratch_shapes=[pltpu.VMEM((SSD_STATE, SSD_INNER), F32)],
        compiler_params=_params(("parallel", "arbitrary")),
        name="ssd_scan",
    )(z, xs, bc, dt, pad_h(dt_bias), pad_h(a_log), d_exp, norm_w.reshape(1, -1), e3)


def _post_kernel(x_ref, a_ref, wo_ref, g_ref, w1_ref, w2_ref, gf_ref, o_ref,
                 *, final_norm, th):
    x1 = x_ref[...] + jnp.dot(a_ref[...], wo_ref[...], preferred_element_type=F32)
    h = _rms(x1, g_ref[...]).astype(BF16)
    y = x1
    for c0 in range(0, w1_ref.shape[1], th):
        u = jnp.maximum(
            jnp.dot(h, w1_ref[:, c0:c0 + th], preferred_element_type=F32), 0.0)
        y = y + jnp.dot((u * u).astype(BF16), w2_ref[c0:c0 + th, :],
                        preferred_element_type=F32)
    if final_norm:
        y = _rms(y, gf_ref[...])
    o_ref[...] = y


def _post(x, a, w_o, g, w1_all, w2_all, layer, g_final, final_norm,
          tm=TM_POST, th=TH_POST):
    t, d = x.shape
    ka = a.shape[1]
    hid = w1_all.shape[2]
    row = lambda i: (i, 0)
    fixed = lambda i: (0, 0)
    resident = lambda shape: pl.BlockSpec(shape, fixed, pipeline_mode=pl.Buffered(1))
    of_layer = lambda shape: pl.BlockSpec((None,) + shape, lambda i: (layer, 0, 0),
                                          pipeline_mode=pl.Buffered(1))
    return pl.pallas_call(
        functools.partial(_post_kernel, final_norm=final_norm, th=th),
        grid=(t // tm,),
        in_specs=[pl.BlockSpec((tm, d), row),
                  pl.BlockSpec((tm, ka), row),
                  resident((ka, d)),
                  resident((1, d)),
                  of_layer((d, hid)),
                  of_layer((hid, d)),
                  resident((1, d))],
        out_specs=pl.BlockSpec((tm, d), row),
        out_shape=jax.ShapeDtypeStruct((t, d), F32),
        compiler_params=_params(("parallel",), VMEM_MIB_LARGE),
        name="post_mlp",
    )(x, a, w_o.astype(BF16), g.reshape(1, d), w1_all, w2_all, g_final.reshape(1, d))


def _mla_mixer(x, g, rope, w_in, q_norm, w_uq, kv_norm, w_ukv, b, s):
    q, kv, kpe = _mla_proj(x, g, w_in, q_norm, w_uq, kv_norm, w_ukv, rope)
    return _mla_attn(q, kv, kpe, b, s)


def _moba_mixer(x, g, w_qkv, bias, b, s):
    nq = MOBA_HEADS * MOBA_HEAD_DIM
    q, kv = _norm_proj(x, g, w_qkv.astype(BF16), (nq, 2 * nq), (BF16, BF16),
                       (MOBA_HEAD_DIM ** -0.5 * LOG2E, None), tm=TM_NORM_PROJ)
    return _moba_attn(q, kv, bias, b, s)


def _ssd_mixer(x, g, w_in, conv_w, conv_b, dt_bias, a_log, d_skip, norm_w, b, s):
    z, xs, bc, dt = _ssd_proj(x, g, w_in, conv_w, conv_b, s)
    return _ssd_scan(z, xs, bc, dt, dt_bias, a_log, d_skip, norm_w, b, s)


def kernel(x, positions, norm_mix, norm_mlp, norm_final, rel_bias, mla_w_in, mla_q_norm, mla_w_uq, mla_kv_norm, mla_w_ukv, mla_w_o, moba_w_qkv, moba_w_o, ssd_w_in, ssd_conv_w, ssd_conv_b, ssd_dt_bias, ssd_a_log, ssd_d, ssd_norm, ssd_w_out, mlp_w1, mlp_w2):
    b, s, d = x.shape
    xf = x.reshape(b * s, d)
    rope = None
    bias = None
    w1_all = mlp_w1.astype(BF16)
    w2_all = mlp_w2.astype(BF16)
    for i in range(DEPTH):
        kind, j = i % N_MIXERS, i // N_MIXERS
        if kind == 0:
            if rope is None:
                rope = _rope_tables(positions)
            a = _mla_mixer(xf, norm_mix[i], rope, mla_w_in[j], mla_q_norm[j], mla_w_uq[j],
                           mla_kv_norm[j], mla_w_ukv[j], b, s)
            w_o = mla_w_o[j]
        elif kind == 1:
            if bias is None:
                bias = _moba_bias(rel_bias)
            a = _moba_mixer(xf, norm_mix[i], moba_w_qkv[j], bias, b, s)
            w_o = moba_w_o[j]
        else:
            a = _ssd_mixer(xf, norm_mix[i], ssd_w_in[j], ssd_conv_w[j], ssd_conv_b[j],
                           ssd_dt_bias[j], ssd_a_log[j], ssd_d[j], ssd_norm[j], b, s)
            w_o = ssd_w_out[j]
        xf = _post(xf, a, w_o, norm_mlp[i], w1_all, w2_all, i, norm_final,
                   final_norm=(i == DEPTH - 1))
    return xf.reshape(b, s, d)
```
